```python
import jax, jax.numpy as jnp
from jax import lax
import numpy as np

D_MODEL = 2048
BATCH = 8
SEQ = 4096
DEPTH = 4

RET_HEADS = 4
RET_DK = 128
RET_DV = 128
RET_WIDTH = RET_HEADS * RET_DV
RET_CHUNK = 128
ROPE_BASE = 10000.0
DIL_HEADS = 6
DIL_HD = 128
DIL_WIDTH = DIL_HEADS * DIL_HD
DIL_PATTERNS = ((128, 1), (512, 4), (2048, 16))
DIL_BLOCK = 128
GLA_HEADS = 6
GLA_DK = 64
GLA_DV = 128
GLA_QK_WIDTH = GLA_HEADS * GLA_DK
GLA_V_WIDTH = GLA_HEADS * GLA_DV
GLA_GATE_RANK = 16
GLA_TAU = 16.0
GLA_CHUNK = 64
MIX_WIDTH = RET_WIDTH + DIL_WIDTH + GLA_V_WIDTH
N_GROUPS = 4
EXPERTS_PER_GROUP = 8
N_EXPERTS = N_GROUPS * EXPERTS_PER_GROUP
TOP_K = 2
D_FF_EXPERT = 512
MOE_BLOCK = 128
LN_EPS = 1e-5
ALPHA = (2 * DEPTH) ** 0.25
BETA = (8 * DEPTH) ** -0.25

SPLIT_SIZES = (RET_WIDTH, RET_WIDTH, RET_WIDTH, RET_WIDTH,
               DIL_WIDTH, DIL_WIDTH, DIL_WIDTH,
               GLA_QK_WIDTH, GLA_QK_WIDTH, GLA_V_WIDTH, GLA_V_WIDTH, GLA_GATE_RANK)
IN_WIDTH = sum(SPLIT_SIZES)
SPLIT_POINTS = tuple(int(v) for v in np.cumsum(SPLIT_SIZES)[:-1])

kernel_name = 'hymba_style_ret_dilated_gla_hmoe_deepnorm'


def layer_norm(x, g, b):
    xf = x.astype(jnp.float32)
    mu = xf.mean(-1, keepdims=True)
    var = jnp.square(xf - mu).mean(-1, keepdims=True)
    return ((xf - mu) * lax.rsqrt(var + LN_EPS)).astype(x.dtype) * g + b


def head_norm(t):
    tf = t.astype(jnp.float32)
    mu = tf.mean(-1, keepdims=True)
    var = jnp.square(tf - mu).mean(-1, keepdims=True)
    return ((tf - mu) * lax.rsqrt(var + LN_EPS)).astype(t.dtype)


def heads(t, n):
    b, s, w = t.shape
    return t.reshape(b, s, n, w // n).transpose(0, 2, 1, 3)


def merge(t):
    b, h, s, d = t.shape
    return t.transpose(0, 2, 1, 3).reshape(b, s, h * d)


def rotate(t, pos):
    half = t.shape[-1] // 2
    inv = ROPE_BASE ** (-jnp.arange(half, dtype=jnp.float32) / half)
    ang = pos.astype(jnp.float32)[:, None] * inv[None, :]
    cos, sin = jnp.cos(ang).astype(t.dtype), jnp.sin(ang).astype(t.dtype)
    t1, t2 = t[..., :half], t[..., half:]
    return jnp.concatenate([t1 * cos - t2 * sin, t1 * sin + t2 * cos], axis=-1)


def retention(q, k, v):
    B, H, S, dk = q.shape
    C = RET_CHUNK
    N = S // C
    log_g = jnp.log1p(-jnp.exp2(-5.0 - jnp.arange(H, dtype=jnp.float32)))
    idx = jnp.arange(C, dtype=jnp.float32)
    diff = idx[:, None] - idx[None, :]
    intra_decay = jnp.where(diff >= 0, jnp.exp(jnp.maximum(diff, 0.0)[None] * log_g[:, None, None]), 0.0).astype(q.dtype)
    k_decay = jnp.exp((C - 1 - idx)[None] * log_g[:, None]).astype(q.dtype)
    q_decay = jnp.exp((idx + 1)[None] * log_g[:, None]).astype(q.dtype)
    chunk_decay = jnp.exp(C * log_g).astype(q.dtype)[None, :, None, None]
    qc = q.reshape(B, H, N, C, dk)
    kc = k.reshape(B, H, N, C, dk)
    vc = v.reshape(B, H, N, C, -1)
    scores = jnp.einsum('bhncd,bhnmd->bhncm', qc, kc) * intra_decay[None, :, None]
    intra = jnp.einsum('bhncm,bhnme->bhnce', scores, vc)
    kv = jnp.einsum('bhncd,bhnce->bhnde', kc * k_decay[None, :, None, :, None], vc)

    def step(state, kv_n):
        return state * chunk_decay + kv_n, state

    _, prev = lax.scan(step, jnp.zeros_like(kv[:, :, 0]), jnp.moveaxis(kv, 2, 0))
    prev = jnp.moveaxis(prev, 0, 2)
    cross = jnp.einsum('bhncd,bhnde->bhnce', qc * q_decay[None, :, None, :, None], prev)
    return (intra + cross).reshape(B, H, S, -1)


def dilated_branch(q, k, v, window, dilation):
    B, H, S, d = q.shape
    W = window // dilation
    Q = DIL_BLOCK
    span = dilation * Q
    Sp = -(-S // span) * span
    L = Sp // dilation
    nb = L // Q

    def to_sub(t):
        t = jnp.pad(t, ((0, 0), (0, 0), (0, Sp - S), (0, 0)))
        t = t.reshape(B, H, L, dilation, d).transpose(0, 1, 3, 2, 4)
        return t.reshape(B, H, dilation, nb, Q, d)

    def with_prev(t):
        prev = jnp.pad(t, ((0, 0), (0, 0), (0, 0), (1, 0), (0, 0), (0, 0)))[:, :, :, :-1]
        return jnp.concatenate([prev, t], axis=4)

    qs = to_sub(q)
    kb = with_prev(to_sub(k))
    vb = with_prev(to_sub(v))
    scores = jnp.einsum('bhrnqd,bhrnkd->bhrnqk', qs, kb, preferred_element_type=jnp.float32) * (d ** -0.5)
    qi = jnp.arange(Q)[:, None] + Q
    kj = jnp.arange(2 * Q)[None, :]
    dist = qi - kj
    band = (dist >= 0) & (dist <= W)
    blk = jnp.arange(nb)[:, None, None]
    valid = band[None] & ((blk > 0) | (kj[None] >= Q))
    scores = jnp.where(valid, scores, jnp.finfo(jnp.float32).min)
    m = scores.max(-1)
    p = jnp.exp(scores - m[..., None])
    s = p.sum(-1)
    o = jnp.einsum('bhrnqk,bhrnkd->bhrnqd', p, vb.astype(jnp.float32)) / s[..., None]

    def from_sub(t):
        extra = t.shape[5:]
        t = t.reshape((B, H, dilation, L) + extra)
        t = jnp.moveaxis(t, 2, 3).reshape((B, H, Sp) + extra)
        return t[:, :, :S]

    return from_sub(o), from_sub(m), from_sub(s)


def dilated_attention(q, k, v):
    outs, maxes, dens = [], [], []
    for window, dilation in DIL_PATTERNS:
        o, m, s = dilated_branch(q, k, v, window, dilation)
        outs.append(o)
        maxes.append(m)
        dens.append(s)
    m_all = jnp.stack(maxes)
    w = jnp.exp(m_all - m_all.max(0, keepdims=True)) * jnp.stack(dens)
    w = w / w.sum(0, keepdims=True)
    out = jnp.einsum('pbhs,pbhsd->bhsd', w, jnp.stack(outs))
    return out.astype(q.dtype)


def gla(q, k, v, log_a):
    B, H, S, dk = q.shape
    dv = v.shape[-1]
    C = GLA_CHUNK
    N = S // C
    f32 = jnp.float32

    def chunks(t):
        return jnp.moveaxis(t.astype(f32).reshape(B, H, N, C, t.shape[-1]), 2, 0)

    causal = jnp.tril(jnp.ones((C, C), dtype=bool))[:, :, None]

    def step(state, inp):
        qc, kc, vc, ac = inp
        b = jnp.cumsum(ac, axis=2)
        dec = jnp.exp(jnp.where(causal, b[:, :, :, None, :] - b[:, :, None, :, :], -jnp.inf))
        A = jnp.einsum('bhtd,bhsd,bhtsd->bhts', qc, kc, dec)
        intra = jnp.einsum('bhts,bhse->bhte', A, vc)
        inter = jnp.einsum('bhtd,bhde->bhte', qc * jnp.exp(b), state)
        b_last = b[:, :, -1:, :]
        new_state = jnp.exp(b_last[:, :, 0, :, None]) * state + jnp.einsum('bhsd,bhse->bhde', kc * jnp.exp(b_last - b), vc)
        return new_state, intra + inter

    state0 = jnp.zeros((B, H, dk, dv), f32)
    _, out = lax.scan(step, state0, (chunks(q), chunks(k), chunks(v), chunks(log_a)))
    return jnp.moveaxis(out, 0, 2).reshape(B, H, S, dv).astype(v.dtype)


def hier_moe(x, w_rg, b_rg, w_re, b_re, w_gate, w_up, w_down):
    B, S, D = x.shape
    T = B * S
    xf = x.reshape(T, D)
    g_logits = (xf @ w_rg + b_rg).astype(jnp.float32)
    g_probs = jax.nn.softmax(g_logits, axis=-1)
    g_sel = jnp.argmax(g_logits, axis=-1)
    g_w = jnp.take_along_axis(g_probs, g_sel[:, None], axis=-1)[:, 0]
    e_logits = (xf @ w_re + b_re).astype(jnp.float32).reshape(T, N_GROUPS, EXPERTS_PER_GROUP)
    sel_logits = jnp.take_along_axis(e_logits, g_sel[:, None, None], axis=1)[:, 0]
    top_v, top_i = lax.top_k(sel_logits, TOP_K)
    gate = g_w[:, None] * jax.nn.softmax(top_v, axis=-1)
    expert_id = g_sel[:, None].astype(jnp.int32) * EXPERTS_PER_GROUP + top_i.astype(jnp.int32)

    n_assign = T * TOP_K
    e_flat = expert_id.reshape(-1)
    w_flat = gate.reshape(-1)
    tok_flat = jnp.arange(n_assign, dtype=jnp.int32) // TOP_K
    order = jnp.argsort(e_flat)
    e_sorted = e_flat[order]
    counts = jnp.bincount(e_flat, length=N_EXPERTS)
    starts = jnp.cumsum(counts) - counts
    pcounts = (counts + MOE_BLOCK - 1) // MOE_BLOCK * MOE_BLOCK
    pends = jnp.cumsum(pcounts)
    pstarts = pends - pcounts
    dest = pstarts[e_sorted] + jnp.arange(n_assign, dtype=jnp.int32) - starts[e_sorted]
    n_rows = -(-(n_assign + N_EXPERTS * (MOE_BLOCK - 1)) // MOE_BLOCK) * MOE_BLOCK
    n_blk = n_rows // MOE_BLOCK
    row_tok = jnp.full((n_rows,), T, jnp.int32).at[dest].set(tok_flat[order])
    row_w = jnp.zeros((n_rows,), x.dtype).at[dest].set(w_flat[order].astype(x.dtype))
    blk_expert = jnp.minimum(jnp.searchsorted(pends, jnp.arange(n_blk) * MOE_BLOCK, side='right'), N_EXPERTS - 1)
    x_pad = jnp.concatenate([xf, jnp.zeros((1, D), x.dtype)], axis=0)

    def expert_block(args):
        rows, wts, e = args
        xb = x_pad[rows]
        h = jax.nn.silu(xb @ w_gate[e]) * (xb @ w_up[e])
        return (h @ w_down[e]) * wts[:, None]

    yb = lax.map(expert_block, (row_tok.reshape(n_blk, MOE_BLOCK), row_w.reshape(n_blk, MOE_BLOCK), blk_expert))
    y = jnp.zeros((T + 1, D), x.dtype).at[row_tok].add(yb.reshape(n_rows, D))[:T]
    return y.reshape(B, S, D)


def setup_inputs(seed: int = 0) -> dict:
    key = jax.random.key(seed)
    ks = jax.random.split(key, 18)
    f32 = jnp.float32

    def nrm(k, shape, scale):
        return jax.random.normal(k, shape, f32) * scale

    return {
        'x': nrm(ks[0], (BATCH, SEQ, D_MODEL), 1.0),
        'w_in': nrm(ks[1], (DEPTH, D_MODEL, IN_WIDTH), D_MODEL ** -0.5),
        'w_gla_gate': nrm(ks[2], (DEPTH, GLA_GATE_RANK, GLA_QK_WIDTH), GLA_GATE_RANK ** -0.5),
        'b_gla_gate': nrm(ks[3], (DEPTH, GLA_QK_WIDTH), 0.1),
        'ret_norm_g': 1.0 + nrm(ks[4], (DEPTH, RET_WIDTH), 0.02),
        'gla_norm_g': 1.0 + nrm(ks[5], (DEPTH, GLA_V_WIDTH), 0.02),
        'w_out': nrm(ks[6], (DEPTH, MIX_WIDTH, D_MODEL), BETA * MIX_WIDTH ** -0.5),
        'ln1_g': 1.0 + nrm(ks[7], (DEPTH, D_MODEL), 0.02),
        'ln1_b': nrm(ks[8], (DEPTH, D_MODEL), 0.02),
        'w_router_group': nrm(ks[9], (DEPTH, D_MODEL, N_GROUPS), D_MODEL ** -0.5),
        'b_router_group': nrm(ks[10], (DEPTH, N_GROUPS), 0.01),
        'w_router_expert': nrm(ks[11], (DEPTH, D_MODEL, N_EXPERTS), D_MODEL ** -0.5),
        'b_router_expert': nrm(ks[12], (DEPTH, N_EXPERTS), 0.01),
        'w_expert_gate': nrm(ks[13], (DEPTH, N_EXPERTS, D_MODEL, D_FF_EXPERT), D_MODEL ** -0.5),
        'w_expert_up': nrm(ks[14], (DEPTH, N_EXPERTS, D_MODEL, D_FF_EXPERT), D_MODEL ** -0.5),
        'w_expert_down': nrm(ks[15], (DEPTH, N_EXPERTS, D_FF_EXPERT, D_MODEL), BETA * D_FF_EXPERT ** -0.5),
        'ln2_g': 1.0 + nrm(ks[16], (DEPTH, D_MODEL), 0.02),
        'ln2_b': nrm(ks[17], (DEPTH, D_MODEL), 0.02),
    }


def reference(x, w_in, w_gla_gate, b_gla_gate, ret_norm_g, gla_norm_g, w_out, ln1_g, ln1_b,
              w_router_group, b_router_group, w_router_expert, b_router_expert,
              w_expert_gate, w_expert_up, w_expert_down, ln2_g, ln2_b):
    S = x.shape[1]
    pos = jnp.arange(S)
    for l in range(DEPTH):
        proj = x @ w_in[l]
        rq, rk, rv, rg, dq, dk, dv, gq, gk, gv, gr, ga = jnp.split(proj, SPLIT_POINTS, axis=-1)
        q_r = rotate(heads(rq, RET_HEADS), pos)
        k_r = rotate(heads(rk, RET_HEADS), pos) * (RET_DK ** -0.5)
        ret = retention(q_r, k_r, heads(rv, RET_HEADS))
        ret = jax.nn.silu(rg) * (merge(head_norm(ret)) * ret_norm_g[l])
        dil = merge(dilated_attention(heads(dq, DIL_HEADS), heads(dk, DIL_HEADS), heads(dv, DIL_HEADS)))
        log_a = jax.nn.log_sigmoid((ga @ w_gla_gate[l] + b_gla_gate[l]).astype(jnp.float32)) / GLA_TAU
        g_o = gla(heads(gq, GLA_HEADS) * (GLA_DK ** -0.5), heads(gk, GLA_HEADS), heads(gv, GLA_HEADS), heads(log_a, GLA_HEADS))
        g_o = jax.nn.silu(gr) * (merge(head_norm(g_o)) * gla_norm_g[l])
        mixed = jnp.concatenate([ret, dil, g_o], axis=-1) @ w_out[l]
        x = layer_norm(ALPHA * x + mixed, ln1_g[l], ln1_b[l])
        moe = hier_moe(x, w_router_group[l], b_router_group[l], w_router_expert[l], b_router_expert[l],
                       w_expert_gate[l], w_expert_up[l], w_expert_down[l])
        x = layer_norm(ALPHA * x + moe, ln2_g[l], ln2_b[l])
    return x
```

```python
import functools

import jax
import jax.numpy as jnp
from jax import lax
from jax.experimental import pallas as pl
from jax.experimental.pallas import tpu as pltpu

F32 = jnp.float32
BF16 = jnp.bfloat16

DEPTH = 4
RET_HEADS = 4
RET_CHUNK = 128
ROPE_BASE = 10000.0
DIL_HEADS = 6
DIL_DILATIONS = (1, 4, 16)
DIL_BLOCK = 128
GLA_HEADS = 6
GLA_DK = 64
GLA_TAU = 16.0
GLA_CHUNK = 64
N_GROUPS = 4
EXPERTS_PER_GROUP = 8
N_EXPERTS = N_GROUPS * EXPERTS_PER_GROUP
TOP_K = 2
LN_EPS = 1e-5
ALPHA = (2 * DEPTH) ** 0.25

LANE = 128
HEAD = 128
OFF_RQ, OFF_RK, OFF_RV, OFF_RG = 0, 4, 8, 12
OFF_DQ, OFF_DK, OFF_DV = 16, 22, 28
OFF_GQ, OFF_GK = 34, 37
OFF_GV, OFF_GR = 20, 23
OFF_GA = 52
PROJ_REAL = 6672
PROJ_PAD = 6912
MOE_BM = 256
VMEM_LIMIT = 48 * 1024 * 1024


def _dot(a, b):
    return jnp.dot(a, b, preferred_element_type=F32)


def _dot_nt(a, b):
    return lax.dot_general(a, b, (((1,), (1,)), ((), ())), preferred_element_type=F32)


def _dot_tn(a, b):
    return lax.dot_general(a, b, (((0,), (0,)), ((), ())), preferred_element_type=F32)


def _params(sem):
    return pltpu.CompilerParams(dimension_semantics=sem, vmem_limit_bytes=VMEM_LIMIT)


def _silu(g):
    return g / (1.0 + jnp.exp(-g))


def _head_norm(o):
    mu = jnp.mean(o, axis=-1, keepdims=True)
    d = o - mu
    var = jnp.mean(d * d, axis=-1, keepdims=True)
    return d * lax.rsqrt(var + LN_EPS)


def _mm_kernel(x_ref, w_ref, o_ref):
    o_ref[...] = _dot(x_ref[...], w_ref[...])


def _in_proj(xb, wb):
    T, K = xb.shape
    N = wb.shape[1]
    tm, tn = 1024, 768
    return pl.pallas_call(
        _mm_kernel,
        grid=(T // tm, N // tn),
        in_specs=[pl.BlockSpec((tm, K), lambda i, j: (i, 0)),
                  pl.BlockSpec((K, tn), lambda i, j: (0, j))],
        out_specs=pl.BlockSpec((tm, tn), lambda i, j: (i, j)),
        out_shape=jax.ShapeDtypeStruct((T, N), F32),
        compiler_params=_params(("parallel", "arbitrary")),
        name="in_proj",
    )(xb, wb)


def _ret_kernel(q_ref, k_ref, v_ref, g_ref, cos_ref, sin_ref, intra_ref, kdec_ref, qdec_ref,
                cdec_ref, gain_ref, o_ref, state_ref):
    S = q_ref.shape[0]
    C = RET_CHUNK
    scale = HEAD ** -0.5
    state_ref[...] = jnp.zeros_like(state_ref)

    def chunk(n, carry):
        rows = pl.ds(pl.multiple_of(n * C, C), C)
        cos = cos_ref[rows, :]
        sin = sin_ref[rows, :]
        q = q_ref[rows, :]
        k = k_ref[rows, :]
        qr = q * cos + pltpu.roll(q, HEAD // 2, 1) * sin
        kr = (k * cos + pltpu.roll(k, HEAD // 2, 1) * sin) * scale
        v = v_ref[rows, :].astype(BF16)
        scores = _dot_nt(qr.astype(BF16), kr.astype(BF16)) * intra_ref[...]
        state = state_ref[...]
        o = _dot(scores.astype(BF16), v) + _dot((qr * qdec_ref[...]).astype(BF16), state.astype(BF16))
        state_ref[...] = state * cdec_ref[...] + _dot_tn((kr * kdec_ref[...]).astype(BF16), v)
        g = g_ref[rows, :]
        o_ref[rows, :] = (_silu(g) * (_head_norm(o) * gain_ref[...])).astype(o_ref.dtype)
        return carry

    lax.fori_loop(0, S // C, chunk, 0)


def _retention(proj, cosf, sinf, tables, gain):
    B, S, _ = proj.shape
    H, C = RET_HEADS, RET_CHUNK
    intra, kdec, qdec, cdec = tables

    def col(off):
        return pl.BlockSpec((None, S, HEAD), lambda b, h: (b, 0, off + h))

    full = pl.BlockSpec((S, HEAD), lambda b, h: (0, 0))
    per_head = lambda r: pl.BlockSpec((None, r, HEAD), lambda b, h: (h, 0, 0))
    return pl.pallas_call(
        _ret_kernel,
        grid=(B, H),
        in_specs=[col(OFF_RQ), col(OFF_RK), col(OFF_RV), col(OFF_RG), full, full,
                  pl.BlockSpec((None, C, C), lambda b, h: (h, 0, 0)),
                  per_head(C), per_head(C), per_head(1), per_head(1)],
        out_specs=pl.BlockSpec((None, S, HEAD), lambda b, h: (b, 0, h)),
        out_shape=jax.ShapeDtypeStruct((B, S, H * HEAD), BF16),
        scratch_shapes=[pltpu.VMEM((HEAD, HEAD), F32)],
        compiler_params=_params(("parallel", "parallel")),
        name="retention",
    )(proj, proj, proj, proj, cosf, sinf, intra, kdec, qdec, cdec, gain)


_DIL_COPY = 256


def _dil_kernel(q_ref, k_ref, v_ref, o_ref, qp, kp, vp, accp, mp, sp, acc, m, s):
    S = q_ref.shape[0]
    Q = DIL_BLOCK
    CH = _DIL_COPY
    scale = HEAD ** -0.5
    neg = jnp.finfo(F32).min
    ii = lax.broadcasted_iota(jnp.int32, (Q, Q), 0)
    jj = lax.broadcasted_iota(jnp.int32, (Q, Q), 1)
    causal = jj <= ii
    anti = jj >= ii

    for pi, d in enumerate(DIL_DILATIONS):
        L = S // d
        nb = L // Q
        for r in range(d):
            for c in range(L // CH):
                src = pl.ds(r + d * c * CH, CH, stride=d) if d > 1 else pl.ds(c * CH, CH)
                dst = pl.ds(r * L + c * CH, CH)
                qp[dst, :] = q_ref[src, :].astype(BF16)
                kp[dst, :] = k_ref[src, :].astype(BF16)
                vp[dst, :] = v_ref[src, :].astype(BF16)

        acc_d, m_d, s_d = (acc, m, s) if pi == 0 else (accp, mp, sp)

        def block(it, carry, nb=nb, acc_d=acc_d, m_d=m_d, s_d=s_d):
            has_prev = (it & (nb - 1)) > 0
            cur = pl.ds(pl.multiple_of(it * Q, Q), Q)
            prev = pl.ds(pl.multiple_of(jnp.maximum(it - 1, 0) * Q, Q), Q)
            qb = qp[cur, :]
            sc = jnp.where(causal, _dot_nt(qb, kp[cur, :]) * scale, neg)
            sv = jnp.where(jnp.logical_and(anti, has_prev), _dot_nt(qb, kp[prev, :]) * scale, neg)
            mx = jnp.maximum(jnp.max(sc, axis=-1, keepdims=True), jnp.max(sv, axis=-1, keepdims=True))
            pc = jnp.exp(sc - mx)
            pv = jnp.exp(sv - mx)
            acc_d[cur, :] = _dot(pc.astype(BF16), vp[cur, :]) + _dot(pv.astype(BF16), vp[prev, :])
            m_d[cur, :] = mx
            s_d[cur, :] = jnp.sum(pc, axis=-1, keepdims=True) + jnp.sum(pv, axis=-1, keepdims=True)
            return carry

        lax.fori_loop(0, S // Q, block, 0)

        if pi > 0:
            for r in range(d):
                for c in range(L // CH):
                    nat = pl.ds(r + d * c * CH, CH, stride=d)
                    per = pl.ds(r * L + c * CH, CH)
                    m_old = m[nat, :]
                    m_new = mp[per, :]
                    mx = jnp.maximum(m_old, m_new)
                    e_old = jnp.exp(m_old - mx)
                    e_new = jnp.exp(m_new - mx)
                    acc[nat, :] = acc[nat, :] * e_old + accp[per, :] * e_new
                    s[nat, :] = s[nat, :] * e_old + sp[per, :] * e_new
                    m[nat, :] = mx

    for c in range(S // CH):
        rows = pl.ds(c * CH, CH)
        o_ref[rows, :] = (acc[rows, :] / s[rows, :]).astype(o_ref.dtype)


def _dilated(proj):
    B, S, _ = proj.shape
    H = DIL_HEADS

    def col(off):
        return pl.BlockSpec((None, S, HEAD), lambda b, h: (b, 0, off + h))

    return pl.pallas_call(
        _dil_kernel,
        grid=(B, H),
        in_specs=[col(OFF_DQ), col(OFF_DK), col(OFF_DV)],
        out_specs=pl.BlockSpec((None, S, HEAD), lambda b, h: (b, 0, h)),
        out_shape=jax.ShapeDtypeStruct((B, S, H * HEAD), BF16),
        scratch_shapes=[pltpu.VMEM((S, HEAD), BF16)] * 3
        + [pltpu.VMEM((S, HEAD), F32), pltpu.VMEM((S, 1), F32), pltpu.VMEM((S, 1), F32)] * 2,
        compiler_params=_params(("parallel", "parallel")),
        name="dilated",
    )(proj, proj, proj)


_GLA_GATE_ROWS = 512


def _gla_kernel(q_ref, k_ref, v_ref, r_ref, ga_ref, wg_ref, bg_ref, gain_ref, o_ref, la_ref, st_ref):
    S = q_ref.shape[0]
    C = GLA_CHUNK
    for c in range(S // _GLA_GATE_ROWS):
        rows = pl.ds(c * _GLA_GATE_ROWS, _GLA_GATE_ROWS)
        z = _dot(ga_ref[rows, :].astype(BF16), wg_ref[...]) + bg_ref[...]
        la_ref[rows, :] = (jnp.minimum(z, 0.0) - jnp.log1p(jnp.exp(-jnp.abs(z)))) * (1.0 / GLA_TAU)
    st_ref[...] = jnp.zeros_like(st_ref)
    row_i = lax.broadcasted_iota(jnp.int32, (C, LANE), 0)
    lane = lax.broadcasted_iota(jnp.int32, (C, LANE), 1)
    tril = lax.broadcasted_iota(jnp.int32, (C, C), 1) <= lax.broadcasted_iota(jnp.int32, (C, C), 0)

    def chunk(n, carry):
        rows = pl.ds(pl.multiple_of(n * C, C), C)
        b = la_ref[rows, :]
        sh = 1
        while sh < C:
            b = b + jnp.where(row_i >= sh, pltpu.roll(b, sh, 0), 0.0)
            sh *= 2
        q = q_ref[rows, :] * (GLA_DK ** -0.5)
        k = k_ref[rows, :]
        b_last = b[C - 1:C, :]
        qe = q * jnp.exp(b)
        ke = (k * jnp.exp(-b)).astype(BF16)
        kd = k * jnp.exp(b_last - b)
        decay = jnp.exp(b_last)
        for hh in range(2):
            mine = jnp.logical_and(lane >= GLA_DK * hh, lane < GLA_DK * (hh + 1))
            cols = pl.ds(HEAD * hh, HEAD)
            qh = jnp.where(mine, qe, 0.0).astype(BF16)
            kh = jnp.where(mine, kd, 0.0).astype(BF16)
            vh = v_ref[rows, cols].astype(BF16)
            a = jnp.where(tril, _dot_nt(qh, ke), 0.0)
            st = st_ref[hh]
            o = _dot(a.astype(BF16), vh) + _dot_nt(qh, st.astype(BF16))
            st_ref[hh] = st * decay + _dot_tn(vh, kh)
            g = r_ref[rows, cols]
            o_ref[rows, cols] = (_silu(g) * (_head_norm(o) * gain_ref[:, cols])).astype(o_ref.dtype)
        return carry

    lax.fori_loop(0, S // C, chunk, 0)


def _gla(proj, wg, bg, gain):
    B, S, _ = proj.shape
    P = GLA_HEADS // 2
    one = lambda off: pl.BlockSpec((None, S, LANE), lambda b, p: (b, 0, off + p))
    two = lambda off: pl.BlockSpec((None, S, 2 * LANE), lambda b, p: (b, 0, off + p))
    return pl.pallas_call(
        _gla_kernel,
        grid=(B, P),
        in_specs=[one(OFF_GQ), one(OFF_GK), two(OFF_GV), two(OFF_GR),
                  pl.BlockSpec((None, S, LANE), lambda b, p: (b, 0, OFF_GA)),
                  pl.BlockSpec((None, LANE, LANE), lambda b, p: (p, 0, 0)),
                  pl.BlockSpec((None, 1, LANE), lambda b, p: (p, 0, 0)),
                  pl.BlockSpec((None, 1, 2 * LANE), lambda b, p: (p, 0, 0))],
        out_specs=pl.BlockSpec((None, S, 2 * LANE), lambda b, p: (b, 0, p)),
        out_shape=jax.ShapeDtypeStruct((B, S, GLA_HEADS * HEAD), BF16),
        scratch_shapes=[pltpu.VMEM((S, LANE), F32), pltpu.VMEM((2, HEAD, LANE), F32)],
        compiler_params=_params(("parallel", "parallel")),
        name="gla",
    )(proj, proj, proj, proj, proj, wg, bg, gain)


def _layer_norm(y, g, b):
    mu = jnp.mean(y, axis=-1, keepdims=True)
    d = y - mu
    var = jnp.mean(d * d, axis=-1, keepdims=True)
    return d * lax.rsqrt(var + LN_EPS) * g + b


def _post_kernel(ret_ref, dil_ref, gla_ref, x_ref, w_ref, g_ref, b_ref, wr_ref, br_ref, x1_ref, rt_ref):
    n_ret = ret_ref.shape[1]
    n_dil = dil_ref.shape[1]
    mixed = (_dot(ret_ref[...], w_ref[0:n_ret, :])
             + _dot(dil_ref[...], w_ref[n_ret:n_ret + n_dil, :])
             + _dot(gla_ref[...], w_ref[n_ret + n_dil:, :]))
    x1 = _layer_norm(ALPHA * x_ref[...] + mixed, g_ref[...], b_ref[...])
    x1_ref[...] = x1

    logits = _dot(x1.astype(BF16), wr_ref[...]) + br_ref[...]
    lane_i = lax.broadcasted_iota(jnp.int32, logits.shape, 1)
    lane = lane_i.astype(F32)
    far = float(LANE)
    ninf = -jnp.inf
    gmask = lane_i < N_GROUPS
    gmax = jnp.max(jnp.where(gmask, logits, ninf), axis=-1, keepdims=True)
    gsel = jnp.min(jnp.where(jnp.logical_and(gmask, logits == gmax), lane, far), axis=-1, keepdims=True)
    g_w = 1.0 / jnp.sum(jnp.where(gmask, jnp.exp(logits - gmax), 0.0), axis=-1, keepdims=True)
    lo = N_GROUPS + EXPERTS_PER_GROUP * gsel
    emask = jnp.logical_and(lane >= lo, lane < lo + EXPERTS_PER_GROUP)
    v1 = jnp.max(jnp.where(emask, logits, ninf), axis=-1, keepdims=True)
    i1 = jnp.min(jnp.where(jnp.logical_and(emask, logits == v1), lane, far), axis=-1, keepdims=True)
    emask2 = jnp.logical_and(emask, lane != i1)
    v2 = jnp.max(jnp.where(emask2, logits, ninf), axis=-1, keepdims=True)
    i2 = jnp.min(jnp.where(jnp.logical_and(emask2, logits == v2), lane, far), axis=-1, keepdims=True)
    t = jnp.exp(v2 - v1)
    gate1 = g_w / (1.0 + t)
    gate2 = g_w * t / (1.0 + t)
    rt_ref[...] = jnp.where(lane_i == 0, i1 - N_GROUPS,
                            jnp.where(lane_i == 1, i2 - N_GROUPS,
                                      jnp.where(lane_i == 2, gate1,
                                                jnp.where(lane_i == 3, gate2, 0.0))))


def _post_attn(ret, dil, gla, x, w, g, b, wr, br):
    T, D = x.shape
    tm = 256
    row = lambda n: pl.BlockSpec((tm, n), lambda i: (i, 0))
    fix = lambda shape: pl.BlockSpec(shape, lambda i: (0, 0))
    return pl.pallas_call(
        _post_kernel,
        grid=(T // tm,),
        in_specs=[row(ret.shape[1]), row(dil.shape[1]), row(gla.shape[1]), row(D),
                  fix(w.shape), fix((1, D)), fix((1, D)), fix(wr.shape), fix((1, LANE))],
        out_specs=[row(D), row(LANE)],
        out_shape=[jax.ShapeDtypeStruct((T, D), F32), jax.ShapeDtypeStruct((T, LANE), F32)],
        compiler_params=_params(("parallel",)),
        name="post_attn",
    )(ret, dil, gla, x, w, g, b, wr, br)


def _row_copy(src_hbm, row, dst, sem):
    return pltpu.make_async_copy(src_hbm.at[pl.ds(row, 1), :], dst, sem)


def _expert_kernel(be_ref, tok_ref, tokn_ref, x_hbm, wg_ref, wu_ref, wd_ref, ys_ref, xbuf, sem):
    del be_ref
    i = pl.program_id(0)
    n = pl.num_programs(0)
    slot = i % 2
    bm = xbuf.shape[1]

    def gather(toks, sl):
        def one(r, carry):
            _row_copy(x_hbm, toks[0, r], xbuf.at[sl, pl.ds(r, 1), :], sem.at[sl]).start()
            return carry
        lax.fori_loop(0, bm, one, 0)

    @pl.when(i == 0)
    def _():
        gather(tok_ref, 0)

    @pl.when(i + 1 < n)
    def _():
        gather(tokn_ref, 1 - slot)

    def wait_one(r, carry):
        _row_copy(x_hbm, 0, xbuf.at[slot, pl.ds(r, 1), :], sem.at[slot]).wait()
        return carry
    lax.fori_loop(0, bm, wait_one, 0)

    xb = xbuf[slot].astype(BF16)
    h = _silu(_dot(xb, wg_ref[...])) * _dot(xb, wu_ref[...])
    ys_ref[...] = _dot(h.astype(BF16), wd_ref[...])


def _experts(x1, row_tok, blk_expert, wg, wu, wd):
    T, D = x1.shape
    F = wg.shape[2]
    nblk = blk_expert.shape[0]
    bm = MOE_BM
    toks = row_tok.reshape(nblk, 1, bm)
    grid_spec = pltpu.PrefetchScalarGridSpec(
        num_scalar_prefetch=1,
        grid=(nblk,),
        in_specs=[
            pl.BlockSpec((None, 1, bm), lambda i, be: (i, 0, 0), memory_space=pltpu.SMEM),
            pl.BlockSpec((None, 1, bm), lambda i, be: (jnp.minimum(i + 1, nblk - 1), 0, 0),
                         memory_space=pltpu.SMEM),
            pl.BlockSpec(memory_space=pl.ANY),
            pl.BlockSpec((None, D, F), lambda i, be: (be[i], 0, 0)),
            pl.BlockSpec((None, D, F), lambda i, be: (be[i], 0, 0)),
            pl.BlockSpec((None, F, D), lambda i, be: (be[i], 0, 0)),
        ],
        out_specs=pl.BlockSpec((bm, D), lambda i, be: (i, 0)),
        scratch_shapes=[pltpu.VMEM((2, bm, D), F32), pltpu.SemaphoreType.DMA((2,))],
    )
    return pl.pallas_call(
        _expert_kernel,
        grid_spec=grid_spec,
        out_shape=jax.ShapeDtypeStruct((nblk * bm, D), F32),
        compiler_params=_params(("arbitrary",)),
        name="experts",
    )(blk_expert, toks, toks, x1, wg, wu, wd)


def _combine_kernel(pos_ref, posn_ref, x1_ref, rt_ref, ys_hbm, g_ref, b_ref, x2_ref, x2b_ref, buf, sem):
    i = pl.program_id(0)
    n = pl.num_programs(0)
    slot = i % 2
    tm = x1_ref.shape[0]

    def gather(pos, sl):
        def one(r, carry):
            for k in range(TOP_K):
                _row_copy(ys_hbm, pos[0, TOP_K * r + k], buf.at[sl, k, pl.ds(r, 1), :], sem.at[sl]).start()
            return carry
        lax.fori_loop(0, tm, one, 0)

    @pl.when(i == 0)
    def _():
        gather(pos_ref, 0)

    @pl.when(i + 1 < n)
    def _():
        gather(posn_ref, 1 - slot)

    def wait_one(r, carry):
        for k in range(TOP_K):
            _row_copy(ys_hbm, 0, buf.at[slot, k, pl.ds(r, 1), :], sem.at[slot]).wait()
        return carry
    lax.fori_loop(0, tm, wait_one, 0)

    rt = rt_ref[...]
    moe = rt[:, 2:3] * buf[slot, 0] + rt[:, 3:4] * buf[slot, 1]
    x2 = _layer_norm(ALPHA * x1_ref[...] + moe, g_ref[...], b_ref[...])
    x2_ref[...] = x2
    x2b_ref[...] = x2.astype(BF16)


def _combine(x1, rt, pos, ys, g, b):
    T, D = x1.shape
    tm = 256
    nt = T // tm
    posr = pos.reshape(nt, 1, TOP_K * tm)
    row = lambda n: pl.BlockSpec((tm, n), lambda i: (i, 0))
    fix = lambda shape: pl.BlockSpec(shape, lambda i: (0, 0))
    return pl.pallas_call(
        _combine_kernel,
        grid=(nt,),
        in_specs=[
            pl.BlockSpec((None, 1, TOP_K * tm), lambda i: (i, 0, 0), memory_space=pltpu.SMEM),
            pl.BlockSpec((None, 1, TOP_K * tm), lambda i: (jnp.minimum(i + 1, nt - 1), 0, 0),
                         memory_space=pltpu.SMEM),
            row(D), row(LANE), pl.BlockSpec(memory_space=pl.ANY), fix((1, D)), fix((1, D)),
        ],
        out_specs=[row(D), row(D)],
        out_shape=[jax.ShapeDtypeStruct((T, D), F32), jax.ShapeDtypeStruct((T, D), BF16)],
        scratch_shapes=[pltpu.VMEM((2, TOP_K, tm, D), F32), pltpu.SemaphoreType.DMA((2,))],
        compiler_params=_params(("arbitrary",)),
        name="combine",
    )(posr, posr, x1, rt, ys, g, b)


def _rope_tables(S):
    half = HEAD // 2
    inv = ROPE_BASE ** (-jnp.arange(half, dtype=F32) / half)
    ang = jnp.arange(S).astype(F32)[:, None] * inv[None, :]
    cos, sin = jnp.cos(ang), jnp.sin(ang)
    return jnp.concatenate([cos, cos], axis=-1), jnp.concatenate([-sin, sin], axis=-1)


def _ret_tables():
    H, C = RET_HEADS, RET_CHUNK
    log_g = jnp.log1p(-jnp.exp2(-5.0 - jnp.arange(H, dtype=F32)))
    idx = jnp.arange(C, dtype=F32)
    diff = idx[:, None] - idx[None, :]
    intra = jnp.where(diff >= 0, jnp.exp(jnp.maximum(diff, 0.0)[None] * log_g[:, None, None]), 0.0)
    kdec = jnp.exp((C - 1 - idx)[None] * log_g[:, None])
    qdec = jnp.exp((idx + 1)[None] * log_g[:, None])
    cdec = jnp.exp(C * log_g)
    wide = lambda t: jnp.broadcast_to(t[:, :, None], t.shape + (HEAD,))
    return intra, wide(kdec), wide(qdec), wide(cdec[:, None])


def _dispatch(rt, n_tok):
    bm = MOE_BM
    n_assign = n_tok * TOP_K
    nblk = n_assign // bm + N_EXPERTS
    e_flat = rt[:, :TOP_K].astype(jnp.int32).reshape(-1)
    onehot = (e_flat[:, None] == jnp.arange(N_EXPERTS, dtype=jnp.int32)[None, :]).astype(jnp.int32)
    csum = jnp.cumsum(onehot, axis=0)
    rank = jnp.sum(csum * onehot, axis=1) - 1
    counts = csum[-1]
    pcounts = (counts + bm - 1) // bm * bm
    pends = jnp.cumsum(pcounts)
    pstarts = pends - pcounts
    dest = jnp.sum(onehot * pstarts[None, :], axis=1) + rank
    tok = jnp.arange(n_assign, dtype=jnp.int32) // TOP_K
    row_tok = jnp.zeros((nblk * bm,), jnp.int32).at[dest].set(tok)
    blk_expert = jnp.minimum(
        jnp.searchsorted(pends, jnp.arange(nblk, dtype=jnp.int32) * bm, side="right"), N_EXPERTS - 1
    ).astype(jnp.int32)
    return row_tok, blk_expert, dest.reshape(n_tok, TOP_K)


def kernel(x, w_in, w_gla_gate, b_gla_gate, ret_norm_g, gla_norm_g, w_out, ln1_g, ln1_b,
           w_router_group, b_router_group, w_router_expert, b_router_expert,
           w_expert_gate, w_expert_up, w_expert_down, ln2_g, ln2_b):
    B, S, D = x.shape
    T = B * S
    assert w_in.shape[2] == PROJ_REAL and S % (DIL_BLOCK * DIL_DILATIONS[-1]) == 0
    cosf, sinf = _rope_tables(S)
    tables = _ret_tables()
    rank = w_gla_gate.shape[1]
    pairs = GLA_HEADS // 2

    xf = x.reshape(T, D)
    xb = xf.astype(BF16)
    for l in range(DEPTH):
        w_in_p = jnp.pad(w_in[l].astype(BF16), ((0, 0), (0, PROJ_PAD - PROJ_REAL)))
        proj = _in_proj(xb, w_in_p).reshape(B, S, PROJ_PAD)

        ret = _retention(proj, cosf, sinf, tables, ret_norm_g[l].reshape(RET_HEADS, 1, HEAD))
        dil = _dilated(proj)
        wg = w_gla_gate[l].reshape(rank, pairs, LANE).transpose(1, 0, 2)
        wg = jnp.pad(wg.astype(BF16), ((0, 0), (0, LANE - rank), (0, 0)))
        gla = _gla(proj, wg, b_gla_gate[l].reshape(pairs, 1, LANE),
                   gla_norm_g[l].reshape(pairs, 1, 2 * HEAD))

        wr = jnp.concatenate([w_router_group[l], w_router_expert[l]], axis=1)
        wr = jnp.pad(wr.astype(BF16), ((0, 0), (0, LANE - wr.shape[1])))
        br = jnp.concatenate([b_router_group[l], b_router_expert[l]])
        br = jnp.pad(br, (0, LANE - br.shape[0])).reshape(1, LANE)
        x1, rt = _post_attn(ret.reshape(T, -1), dil.reshape(T, -1), gla.reshape(T, -1), xf,
                            w_out[l].astype(BF16), ln1_g[l].reshape(1, D), ln1_b[l].reshape(1, D), wr, br)

        row_tok, blk_expert, pos = _dispatch(rt, T)
        ys = _experts(x1, row_tok, blk_expert, w_expert_gate[l].astype(BF16),
                      w_expert_up[l].astype(BF16), w_expert_down[l].astype(BF16))
        xf, xb = _combine(x1, rt, pos, ys, ln2_g[l].reshape(1, D), ln2_b[l].reshape(1, D))
    return xf.reshape(B, S, D)
```

```python
import functools

import jax
import jax.numpy as jnp
from jax import lax
from jax.experimental import pallas as pl
from jax.experimental.pallas import tpu as pltpu

F32 = jnp.float32
BF16 = jnp.bfloat16

DEPTH = 4
RET_HEADS = 4
RET_CHUNK = 128
ROPE_BASE = 10000.0
DIL_HEADS = 6
DIL_DILATIONS = (1, 4, 16)
DIL_BLOCK = 128
GLA_HEADS = 6
GLA_DK = 64
GLA_TAU = 16.0
GLA_CHUNK = 64
N_GROUPS = 4
EXPERTS_PER_GROUP = 8
N_EXPERTS = N_GROUPS * EXPERTS_PER_GROUP
TOP_K = 2
LN_EPS = 1e-5
ALPHA = (2 * DEPTH) ** 0.25

LANE = 128
HEAD = 128
OFF_RQ, OFF_RK, OFF_RV, OFF_RG = 0, 4, 8, 12
OFF_DQ, OFF_DK, OFF_DV = 16, 22, 28
OFF_GQ, OFF_GK = 34, 37
OFF_GV, OFF_GR = 20, 23
OFF_GA = 52
PROJ_REAL = 6672
PROJ_PAD = 6912
MOE_BM = 256
VMEM_LIMIT = 48 * 1024 * 1024


def _dot(a, b):
    return jnp.dot(a, b, preferred_element_type=F32)


def _dot_nt(a, b):
    return lax.dot_general(a, b, (((1,), (1,)), ((), ())), preferred_element_type=F32)


def _dot_tn(a, b):
    return lax.dot_general(a, b, (((0,), (0,)), ((), ())), preferred_element_type=F32)


def _params(sem):
    return pltpu.CompilerParams(dimension_semantics=sem, vmem_limit_bytes=VMEM_LIMIT)


def _silu(g):
    return g / (1.0 + jnp.exp(-g))


def _head_norm(o):
    mu = jnp.mean(o, axis=-1, keepdims=True)
    d = o - mu
    var = jnp.mean(d * d, axis=-1, keepdims=True)
    return d * lax.rsqrt(var + LN_EPS)


def _mm_kernel(x_ref, w_ref, o_ref):
    o_ref[...] = _dot(x_ref[...], w_ref[...])


def _in_proj(xb, wb):
    T, K = xb.shape
    N = wb.shape[1]
    tm, tn = 1024, 768
    return pl.pallas_call(
        _mm_kernel,
        grid=(T // tm, N // tn),
        in_specs=[pl.BlockSpec((tm, K), lambda i, j: (i, 0)),
                  pl.BlockSpec((K, tn), lambda i, j: (0, j))],
        out_specs=pl.BlockSpec((tm, tn), lambda i, j: (i, j)),
        out_shape=jax.ShapeDtypeStruct((T, N), F32),
        compiler_params=_params(("parallel", "arbitrary")),
        name="in_proj",
    )(xb, wb)


def _ret_kernel(q_ref, k_ref, v_ref, g_ref, cos_ref, sin_ref, intra_ref, kdec_ref, qdec_ref,
                cdec_ref, gain_ref, o_ref, state_ref):
    S = q_ref.shape[0]
    C = RET_CHUNK
    scale = HEAD ** -0.5
    state_ref[...] = jnp.zeros_like(state_ref)

    def chunk(n, carry):
        rows = pl.ds(pl.multiple_of(n * C, C), C)
        cos = cos_ref[rows, :]
        sin = sin_ref[rows, :]
        q = q_ref[rows, :]
        k = k_ref[rows, :]
        qr = q * cos + pltpu.roll(q, HEAD // 2, 1) * sin
        kr = (k * cos + pltpu.roll(k, HEAD // 2, 1) * sin) * scale
        v = v_ref[rows, :].astype(BF16)
        scores = _dot_nt(qr.astype(BF16), kr.astype(BF16)) * intra_ref[...]
        state = state_ref[...]
        o = _dot(scores.astype(BF16), v) + _dot((qr * qdec_ref[...]).astype(BF16), state.astype(BF16))
        state_ref[...] = state * cdec_ref[...] + _dot_tn((kr * kdec_ref[...]).astype(BF16), v)
        g = g_ref[rows, :]
        o_ref[rows, :] = (_silu(g) * (_head_norm(o) * gain_ref[...])).astype(o_ref.dtype)
        return carry

    lax.fori_loop(0, S // C, chunk, 0, unroll=2)


def _retention(proj, cosf, sinf, tables, gain):
    B, S, _ = proj.shape
    H, C = RET_HEADS, RET_CHUNK
    intra, kdec, qdec, cdec = tables

    def col(off):
        return pl.BlockSpec((None, S, HEAD), lambda b, h: (b, 0, off + h))

    full = pl.BlockSpec((S, HEAD), lambda b, h: (0, 0))
    per_head = lambda r: pl.BlockSpec((None, r, HEAD), lambda b, h: (h, 0, 0))
    return pl.pallas_call(
        _ret_kernel,
        grid=(B, H),
        in_specs=[col(OFF_RQ), col(OFF_RK), col(OFF_RV), col(OFF_RG), full, full,
                  pl.BlockSpec((None, C, C), lambda b, h: (h, 0, 0)),
                  per_head(C), per_head(C), per_head(1), per_head(1)],
        out_specs=pl.BlockSpec((None, S, HEAD), lambda b, h: (b, 0, h)),
        out_shape=jax.ShapeDtypeStruct((B, S, H * HEAD), BF16),
        scratch_shapes=[pltpu.VMEM((HEAD, HEAD), F32)],
        compiler_params=_params(("parallel", "parallel")),
        name="retention",
    )(proj, proj, proj, proj, cosf, sinf, intra, kdec, qdec, cdec, gain)


def _dil_kernel(q_ref, k_ref, v_ref, o_ref, acc, m, s):
    S = q_ref.shape[0]
    Q = DIL_BLOCK
    scale = HEAD ** -0.5
    neg = jnp.finfo(F32).min
    ii = lax.broadcasted_iota(jnp.int32, (Q, 2 * Q), 0)
    jj = lax.broadcasted_iota(jnp.int32, (Q, 2 * Q), 1)
    cur_ok = jnp.logical_and(jj >= Q, jj - Q <= ii)
    prev_ok = jnp.logical_and(jj < Q, jj >= ii)
    order = tuple(reversed(DIL_DILATIONS))

    for pi, d in enumerate(order):
        nb = S // (d * Q)
        shift = nb.bit_length() - 1
        first, last = pi == 0, pi == len(order) - 1

        def load(ref, start, d=d):
            if d == 1:
                return ref[pl.ds(pl.multiple_of(start, Q), Q), :]
            return ref.at[pl.ds(start, (Q - 1) * d + 1), :][pl.ds(0, Q, stride=d), :]

        def store(ref, start, val, d=d):
            if d == 1:
                ref[pl.ds(pl.multiple_of(start, Q), Q), :] = val
            else:
                ref.at[pl.ds(start, (Q - 1) * d + 1), :][pl.ds(0, Q, stride=d), :] = val

        def pair(jt, carry, d=d, nb=nb, shift=shift, first=first, last=last, load=load, store=store):
            it = 2 * jt
            n0 = it & (nb - 1)
            r = it >> shift
            start = r + d * Q * n0
            has_prev = n0 > 0
            starts = (jnp.maximum(start - d * Q, r), start, start + d * Q)
            ks = [load(k_ref, st).astype(BF16) for st in starts]
            vs = [load(v_ref, st).astype(BF16) for st in starts]
            for b in range(2):
                qs = starts[b + 1]
                q = load(q_ref, qs).astype(BF16)
                sc = _dot_nt(q, jnp.concatenate([ks[b], ks[b + 1]], axis=0)) * scale
                ok = jnp.logical_or(cur_ok, jnp.logical_and(prev_ok, has_prev)) if b == 0 else jnp.logical_or(cur_ok, prev_ok)
                sc = jnp.where(ok, sc, neg)
                mx = jnp.max(sc, axis=-1, keepdims=True)
                p = jnp.exp(sc - mx)
                a = _dot(p.astype(BF16), jnp.concatenate([vs[b], vs[b + 1]], axis=0))
                m_new = jnp.broadcast_to(mx, (Q, HEAD))
                s_new = jnp.broadcast_to(jnp.sum(p, axis=-1, keepdims=True), (Q, HEAD))
                if first:
                    store(acc, qs, a)
                    store(m, qs, m_new)
                    store(s, qs, s_new)
                    continue
                m_old = load(m, qs)
                m_all = jnp.maximum(m_old, m_new)
                e_old = jnp.exp(m_old - m_all)
                e_new = jnp.exp(m_new - m_all)
                a = load(acc, qs) * e_old + a * e_new
                s_all = load(s, qs) * e_old + s_new * e_new
                if last:
                    store(o_ref, qs, (a / s_all).astype(o_ref.dtype))
                else:
                    store(acc, qs, a)
                    store(m, qs, m_all)
                    store(s, qs, s_all)
            return carry

        lax.fori_loop(0, S // (2 * Q), pair, 0, unroll=2)


def _dilated(proj):
    B, S, _ = proj.shape
    H = DIL_HEADS

    def col(off):
        return pl.BlockSpec((None, S, HEAD), lambda b, h: (b, 0, off + h))

    return pl.pallas_call(
        _dil_kernel,
        grid=(B, H),
        in_specs=[col(OFF_DQ), col(OFF_DK), col(OFF_DV)],
        out_specs=pl.BlockSpec((None, S, HEAD), lambda b, h: (b, 0, h)),
        out_shape=jax.ShapeDtypeStruct((B, S, H * HEAD), BF16),
        scratch_shapes=[pltpu.VMEM((S, HEAD), F32)] * 3,
        compiler_params=_params(("parallel", "parallel")),
        name="dilated",
    )(proj, proj, proj)


_GLA_GATE_ROWS = 512


def _gla_kernel(q_ref, k_ref, v_ref, r_ref, ga_ref, wg_ref, bg_ref, gain_ref, o_ref, la_ref, st_ref):
    S = q_ref.shape[0]
    C = GLA_CHUNK
    for c in range(S // _GLA_GATE_ROWS):
        rows = pl.ds(c * _GLA_GATE_ROWS, _GLA_GATE_ROWS)
        z = _dot(ga_ref[rows, :].astype(BF16), wg_ref[...]) + bg_ref[...]
        la_ref[rows, :] = (jnp.minimum(z, 0.0) - jnp.log1p(jnp.exp(-jnp.abs(z)))) * (1.0 / GLA_TAU)
    st_ref[...] = jnp.zeros_like(st_ref)
    row_i = lax.broadcasted_iota(jnp.int32, (C, LANE), 0)
    lane = lax.broadcasted_iota(jnp.int32, (C, LANE), 1)
    tril = lax.broadcasted_iota(jnp.int32, (C, C), 1) <= lax.broadcasted_iota(jnp.int32, (C, C), 0)

    def chunk(n, carry):
        rows = pl.ds(pl.multiple_of(n * C, C), C)
        b = la_ref[rows, :]
        sh = 1
        while sh < C:
            b = b + jnp.where(row_i >= sh, pltpu.roll(b, sh, 0), 0.0)
            sh *= 2
        q = q_ref[rows, :] * (GLA_DK ** -0.5)
        k = k_ref[rows, :]
        b_last = b[C - 1:C, :]
        qe = q * jnp.exp(b)
        ke = (k * jnp.exp(-b)).astype(BF16)
        kd = k * jnp.exp(b_last - b)
        decay = jnp.exp(b_last)
        for hh in range(2):
            mine = jnp.logical_and(lane >= GLA_DK * hh, lane < GLA_DK * (hh + 1))
            cols = pl.ds(HEAD * hh, HEAD)
            qh = jnp.where(mine, qe, 0.0).astype(BF16)
            kh = jnp.where(mine, kd, 0.0).astype(BF16)
            vh = v_ref[rows, cols].astype(BF16)
            a = jnp.where(tril, _dot_nt(qh, ke), 0.0)
            st = st_ref[hh]
            o = _dot(a.astype(BF16), vh) + _dot_nt(qh, st.astype(BF16))
            st_ref[hh] = st * decay + _dot_tn(vh, kh)
            g = r_ref[rows, cols]
            o_ref[rows, cols] = (_silu(g) * (_head_norm(o) * gain_ref[:, cols])).astype(o_ref.dtype)
        return carry

    lax.fori_loop(0, S // C, chunk, 0, unroll=2)


def _gla(proj, wg, bg, gain):
    B, S, _ = proj.shape
    P = GLA_HEADS // 2
    one = lambda off: pl.BlockSpec((None, S, LANE), lambda b, p: (b, 0, off + p))
    two = lambda off: pl.BlockSpec((None, S, 2 * LANE), lambda b, p: (b, 0, off + p))
    return pl.pallas_call(
        _gla_kernel,
        grid=(B, P),
        in_specs=[one(OFF_GQ), one(OFF_GK), two(OFF_GV), two(OFF_GR),
                  pl.BlockSpec((None, S, LANE), lambda b, p: (b, 0, OFF_GA)),
                  pl.BlockSpec((None, LANE, LANE), lambda b, p: (p, 0, 0)),
                  pl.BlockSpec((None, 1, LANE), lambda b, p: (p, 0, 0)),
                  pl.BlockSpec((None, 1, 2 * LANE), lambda b, p: (p, 0, 0))],
        out_specs=pl.BlockSpec((None, S, 2 * LANE), lambda b, p: (b, 0, p)),
        out_shape=jax.ShapeDtypeStruct((B, S, GLA_HEADS * HEAD), BF16),
        scratch_shapes=[pltpu.VMEM((S, LANE), F32), pltpu.VMEM((2, HEAD, LANE), F32)],
        compiler_params=_params(("parallel", "parallel")),
        name="gla",
    )(proj, proj, proj, proj, proj, wg, bg, gain)


def _layer_norm(y, g, b):
    mu = jnp.mean(y, axis=-1, keepdims=True)
    d = y - mu
    var = jnp.mean(d * d, axis=-1, keepdims=True)
    return d * lax.rsqrt(var + LN_EPS) * g + b


SLAB = 16


def _to_slab(ref, val):
    n = val.shape[0]
    for j in range(SLAB):
        ref[pl.ds(j, n, stride=SLAB), :] = val[:, j * LANE:(j + 1) * LANE]


def _from_slab(ref, n):
    return jnp.concatenate([ref[pl.ds(j, n, stride=SLAB), :] for j in range(SLAB)], axis=1)


def _post_kernel(ret_ref, dil_ref, gla_ref, x_ref, w_ref, g_ref, b_ref, wr_ref, br_ref, x1_ref, rt_ref):
    n_ret = ret_ref.shape[1]
    n_dil = dil_ref.shape[1]
    mixed = (_dot(ret_ref[...], w_ref[0:n_ret, :])
             + _dot(dil_ref[...], w_ref[n_ret:n_ret + n_dil, :])
             + _dot(gla_ref[...], w_ref[n_ret + n_dil:, :]))
    x1 = _layer_norm(ALPHA * x_ref[...] + mixed, g_ref[...], b_ref[...])
    _to_slab(x1_ref, x1)

    logits = _dot(x1.astype(BF16), wr_ref[...]) + br_ref[...]
    lane_i = lax.broadcasted_iota(jnp.int32, logits.shape, 1)
    lane = lane_i.astype(F32)
    far = float(LANE)
    ninf = -jnp.inf
    gmask = lane_i < N_GROUPS
    gmax = jnp.max(jnp.where(gmask, logits, ninf), axis=-1, keepdims=True)
    gsel = jnp.min(jnp.where(jnp.logical_and(gmask, logits == gmax), lane, far), axis=-1, keepdims=True)
    g_w = 1.0 / jnp.sum(jnp.where(gmask, jnp.exp(logits - gmax), 0.0), axis=-1, keepdims=True)
    lo = N_GROUPS + EXPERTS_PER_GROUP * gsel
    emask = jnp.logical_and(lane >= lo, lane < lo + EXPERTS_PER_GROUP)
    v1 = jnp.max(jnp.where(emask, logits, ninf), axis=-1, keepdims=True)
    i1 = jnp.min(jnp.where(jnp.logical_and(emask, logits == v1), lane, far), axis=-1, keepdims=True)
    emask2 = jnp.logical_and(emask, lane != i1)
    v2 = jnp.max(jnp.where(emask2, logits, ninf), axis=-1, keepdims=True)
    i2 = jnp.min(jnp.where(jnp.logical_and(emask2, logits == v2), lane, far), axis=-1, keepdims=True)
    t = jnp.exp(v2 - v1)
    gate1 = g_w / (1.0 + t)
    gate2 = g_w * t / (1.0 + t)
    rt_ref[...] = jnp.where(lane_i == 0, i1 - N_GROUPS,
                            jnp.where(lane_i == 1, i2 - N_GROUPS,
                                      jnp.where(lane_i == 2, gate1,
                                                jnp.where(lane_i == 3, gate2, 0.0))))


def _post_attn(ret, dil, gla, x, w, g, b, wr, br):
    T, D = x.shape
    tm = 256
    row = lambda n: pl.BlockSpec((tm, n), lambda i: (i, 0))
    fix = lambda shape: pl.BlockSpec(shape, lambda i: (0, 0))
    return pl.pallas_call(
        _post_kernel,
        grid=(T // tm,),
        in_specs=[row(ret.shape[1]), row(dil.shape[1]), row(gla.shape[1]), row(D),
                  fix(w.shape), fix((1, D)), fix((1, D)), fix(wr.shape), fix((1, LANE))],
        out_specs=[pl.BlockSpec((tm * SLAB, LANE), lambda i: (i, 0)), row(LANE)],
        out_shape=[jax.ShapeDtypeStruct((T * SLAB, LANE), F32), jax.ShapeDtypeStruct((T, LANE), F32)],
        compiler_params=_params(("parallel",)),
        name="post_attn",
    )(ret, dil, gla, x, w, g, b, wr, br)


def _slab_copy(src_hbm, row, dst, r, sem):
    return pltpu.make_async_copy(src_hbm.at[pl.ds(pl.multiple_of(row * SLAB, SLAB), SLAB), :],
                                 dst.at[pl.ds(r * SLAB, SLAB), :], sem)


def _expert_kernel(be_ref, tok_ref, tokn_ref, x_hbm, wg_ref, wu_ref, wd_ref, ys_ref, xbuf, xb_ref, sem):
    del be_ref
    i = pl.program_id(0)
    n = pl.num_programs(0)
    slot = i % 2
    bm = xb_ref.shape[0]

    def gather(toks, sl):
        for r in range(bm):
            _slab_copy(x_hbm, toks[0, r], xbuf.at[sl], r, sem.at[sl]).start()

    def wait(sl):
        for r in range(bm):
            _slab_copy(x_hbm, 0, xbuf.at[sl], r, sem.at[sl]).wait()

    @pl.when(i == 0)
    def _():
        gather(tok_ref, 0)

    wait(slot)
    xb_ref[...] = _from_slab(xbuf.at[slot], bm).astype(BF16)
    gather(tokn_ref, 1 - slot)

    xb = xb_ref[...]
    h = _silu(_dot(xb, wg_ref[...])) * _dot(xb, wu_ref[...])
    _to_slab(ys_ref, _dot(h.astype(BF16), wd_ref[...]))

    @pl.when(i == n - 1)
    def _():
        wait(1 - slot)


def _experts(x1, row_tok, blk_expert, wg, wu, wd):
    D = SLAB * LANE
    F = wg.shape[2]
    nblk = blk_expert.shape[0]
    bm = MOE_BM
    toks = row_tok.reshape(nblk, 1, bm)
    grid_spec = pltpu.PrefetchScalarGridSpec(
        num_scalar_prefetch=1,
        grid=(nblk,),
        in_specs=[
            pl.BlockSpec((None, 1, bm), lambda i, be: (i, 0, 0), memory_space=pltpu.SMEM),
            pl.BlockSpec((None, 1, bm), lambda i, be: (jnp.minimum(i + 1, nblk - 1), 0, 0),
                         memory_space=pltpu.SMEM),
            pl.BlockSpec(memory_space=pl.ANY),
            pl.BlockSpec((None, D, F), lambda i, be: (be[i], 0, 0)),
            pl.BlockSpec((None, D, F), lambda i, be: (be[i], 0, 0)),
            pl.BlockSpec((None, F, D), lambda i, be: (be[i], 0, 0)),
        ],
        out_specs=pl.BlockSpec((bm * SLAB, LANE), lambda i, be: (i, 0)),
        scratch_shapes=[pltpu.VMEM((2, bm * SLAB, LANE), F32), pltpu.VMEM((bm, D), BF16),
                        pltpu.SemaphoreType.DMA((2,))],
    )
    return pl.pallas_call(
        _expert_kernel,
        grid_spec=grid_spec,
        out_shape=jax.ShapeDtypeStruct((nblk * bm * SLAB, LANE), F32),
        compiler_params=_params(("arbitrary",)),
        name="experts",
    )(blk_expert, toks, toks, x1, wg, wu, wd)


def _combine_kernel(pos_ref, posn_ref, x1_ref, rt_ref, ys_hbm, g_ref, b_ref, x2_ref, x2b_ref, buf, sem):
    i = pl.program_id(0)
    n = pl.num_programs(0)
    slot = i % 2
    tm = rt_ref.shape[0]

    def gather(pos, sl):
        for r in range(tm):
            for k in range(TOP_K):
                _slab_copy(ys_hbm, pos[0, TOP_K * r + k], buf.at[sl, k], r, sem.at[sl]).start()

    def wait(sl):
        for r in range(tm):
            for k in range(TOP_K):
                _slab_copy(ys_hbm, 0, buf.at[sl, k], r, sem.at[sl]).wait()

    @pl.when(i == 0)
    def _():
        gather(pos_ref, 0)

    wait(slot)
    rt = rt_ref[...]
    moe = rt[:, 2:3] * _from_slab(buf.at[slot, 0], tm) + rt[:, 3:4] * _from_slab(buf.at[slot, 1], tm)
    gather(posn_ref, 1 - slot)

    x2 = _layer_norm(ALPHA * _from_slab(x1_ref, tm) + moe, g_ref[...], b_ref[...])
    x2_ref[...] = x2
    x2b_ref[...] = x2.astype(BF16)

    @pl.when(i == n - 1)
    def _():
        wait(1 - slot)


def _combine(x1, rt, pos, ys, g, b):
    T = rt.shape[0]
    D = SLAB * LANE
    tm = 128
    nt = T // tm
    posr = pos.reshape(nt, 1, TOP_K * tm)
    row = lambda n: pl.BlockSpec((tm, n), lambda i: (i, 0))
    fix = lambda shape: pl.BlockSpec(shape, lambda i: (0, 0))
    return pl.pallas_call(
        _combine_kernel,
        grid=(nt,),
        in_specs=[
            pl.BlockSpec((None, 1, TOP_K * tm), lambda i: (i, 0, 0), memory_space=pltpu.SMEM),
            pl.BlockSpec((None, 1, TOP_K * tm), lambda i: (jnp.minimum(i + 1, nt - 1), 0, 0),
                         memory_space=pltpu.SMEM),
            pl.BlockSpec((tm * SLAB, LANE), lambda i: (i, 0)), row(LANE),
            pl.BlockSpec(memory_space=pl.ANY), fix((1, D)), fix((1, D)),
        ],
        out_specs=[row(D), row(D)],
        out_shape=[jax.ShapeDtypeStruct((T, D), F32), jax.ShapeDtypeStruct((T, D), BF16)],
        scratch_shapes=[pltpu.VMEM((2, TOP_K, tm * SLAB, LANE), F32), pltpu.SemaphoreType.DMA((2,))],
        compiler_params=_params(("arbitrary",)),
        name="combine",
    )(posr, posr, x1, rt, ys, g, b)


def _rope_tables(S):
    half = HEAD // 2
    inv = ROPE_BASE ** (-jnp.arange(half, dtype=F32) / half)
    ang = jnp.arange(S).astype(F32)[:, None] * inv[None, :]
    cos, sin = jnp.cos(ang), jnp.sin(ang)
    return jnp.concatenate([cos, cos], axis=-1), jnp.concatenate([-sin, sin], axis=-1)


def _ret_tables():
    H, C = RET_HEADS, RET_CHUNK
    log_g = jnp.log1p(-jnp.exp2(-5.0 - jnp.arange(H, dtype=F32)))
    idx = jnp.arange(C, dtype=F32)
    diff = idx[:, None] - idx[None, :]
    intra = jnp.where(diff >= 0, jnp.exp(jnp.maximum(diff, 0.0)[None] * log_g[:, None, None]), 0.0)
    kdec = jnp.exp((C - 1 - idx)[None] * log_g[:, None])
    qdec = jnp.exp((idx + 1)[None] * log_g[:, None])
    cdec = jnp.exp(C * log_g)
    wide = lambda t: jnp.broadcast_to(t[:, :, None], t.shape + (HEAD,))
    return intra, wide(kdec), wide(qdec), wide(cdec[:, None])


def _dispatch(rt, n_tok):
    bm = MOE_BM
    n_assign = n_tok * TOP_K
    nblk = n_assign // bm + N_EXPERTS
    e_flat = rt[:, :TOP_K].astype(jnp.int32).reshape(-1)
    onehot = (e_flat[:, None] == jnp.arange(N_EXPERTS, dtype=jnp.int32)[None, :]).astype(jnp.int32)
    csum = jnp.cumsum(onehot, axis=0)
    rank = jnp.sum(csum * onehot, axis=1) - 1
    counts = csum[-1]
    pcounts = (counts + bm - 1) // bm * bm
    pends = jnp.cumsum(pcounts)
    pstarts = pends - pcounts
    dest = jnp.sum(onehot * pstarts[None, :], axis=1) + rank
    tok = jnp.arange(n_assign, dtype=jnp.int32) // TOP_K
    row_tok = jnp.zeros((nblk * bm,), jnp.int32).at[dest].set(tok)
    blk_expert = jnp.minimum(
        jnp.searchsorted(pends, jnp.arange(nblk, dtype=jnp.int32) * bm, side="right"), N_EXPERTS - 1
    ).astype(jnp.int32)
    return row_tok, blk_expert, dest.reshape(n_tok, TOP_K)


def kernel(x, w_in, w_gla_gate, b_gla_gate, ret_norm_g, gla_norm_g, w_out, ln1_g, ln1_b,
           w_router_group, b_router_group, w_router_expert, b_router_expert,
           w_expert_gate, w_expert_up, w_expert_down, ln2_g, ln2_b):
    B, S, D = x.shape
    T = B * S
    assert w_in.shape[2] == PROJ_REAL and S % (DIL_BLOCK * DIL_DILATIONS[-1]) == 0
    cosf, sinf = _rope_tables(S)
    tables = _ret_tables()
    rank = w_gla_gate.shape[1]
    pairs = GLA_HEADS // 2

    xf = x.reshape(T, D)
    xb = xf.astype(BF16)
    for l in range(DEPTH):
        w_in_p = jnp.pad(w_in[l].astype(BF16), ((0, 0), (0, PROJ_PAD - PROJ_REAL)))
        proj = _in_proj(xb, w_in_p).reshape(B, S, PROJ_PAD)

        ret = _retention(proj, cosf, sinf, tables, ret_norm_g[l].reshape(RET_HEADS, 1, HEAD))
        dil = _dilated(proj)
        wg = w_gla_gate[l].reshape(rank, pairs, LANE).transpose(1, 0, 2)
        wg = jnp.pad(wg.astype(BF16), ((0, 0), (0, LANE - rank), (0, 0)))
        gla = _gla(proj, wg, b_gla_gate[l].reshape(pairs, 1, LANE),
                   gla_norm_g[l].reshape(pairs, 1, 2 * HEAD))

        wr = jnp.concatenate([w_router_group[l], w_router_expert[l]], axis=1)
        wr = jnp.pad(wr.astype(BF16), ((0, 0), (0, LANE - wr.shape[1])))
        br = jnp.concatenate([b_router_group[l], b_router_expert[l]])
        br = jnp.pad(br, (0, LANE - br.shape[0])).reshape(1, LANE)
        x1, rt = _post_attn(ret.reshape(T, -1), dil.reshape(T, -1), gla.reshape(T, -1), xf,
                            w_out[l].astype(BF16), ln1_g[l].reshape(1, D), ln1_b[l].reshape(1, D), wr, br)

        row_tok, blk_expert, pos = _dispatch(rt, T)
        ys = _experts(x1, row_tok, blk_expert, w_expert_gate[l].astype(BF16),
                      w_expert_up[l].astype(BF16), w_expert_down[l].astype(BF16))
        xf, xb = _combine(x1, rt, pos, ys, ln2_g[l].reshape(1, D), ln2_b[l].reshape(1, D))
    return xf.reshape(B, S, D)
```

```python
import functools

import jax
import jax.numpy as jnp
from jax import lax
from jax.experimental import pallas as pl
from jax.experimental.pallas import tpu as pltpu

F32 = jnp.float32
BF16 = jnp.bfloat16

DEPTH = 4
RET_HEADS = 4
RET_CHUNK = 128
ROPE_BASE = 10000.0
DIL_HEADS = 6
DIL_DILATIONS = (1, 4, 16)
DIL_BLOCK = 128
GLA_HEADS = 6
GLA_DK = 64
GLA_TAU = 16.0
GLA_CHUNK = 64
N_GROUPS = 4
EXPERTS_PER_GROUP = 8
N_EXPERTS = N_GROUPS * EXPERTS_PER_GROUP
TOP_K = 2
LN_EPS = 1e-5
ALPHA = (2 * DEPTH) ** 0.25

LANE = 128
HEAD = 128
OFF_RQ, OFF_RK, OFF_RV, OFF_RG = 0, 4, 8, 12
OFF_DQ, OFF_DK, OFF_DV = 16, 22, 28
OFF_GQ, OFF_GK = 34, 37
OFF_GV, OFF_GR = 20, 23
OFF_GA = 52
PROJ_REAL = 6672
PROJ_PAD = 6912
MOE_BM = 256
VMEM_LIMIT = 48 * 1024 * 1024


def _dot(a, b):
    return jnp.dot(a, b, preferred_element_type=F32)


def _dot_nt(a, b):
    return lax.dot_general(a, b, (((1,), (1,)), ((), ())), preferred_element_type=F32)


def _dot_tn(a, b):
    return lax.dot_general(a, b, (((0,), (0,)), ((), ())), preferred_element_type=F32)


def _params(sem):
    return pltpu.CompilerParams(dimension_semantics=sem, vmem_limit_bytes=VMEM_LIMIT)


def _silu(g):
    return g / (1.0 + jnp.exp(-g))


def _head_norm(o):
    mu = jnp.mean(o, axis=-1, keepdims=True)
    d = o - mu
    var = jnp.mean(d * d, axis=-1, keepdims=True)
    return d * lax.rsqrt(var + LN_EPS)


def _mm_kernel(x_ref, w_ref, o_ref):
    o_ref[...] = _dot(x_ref[...], w_ref[...])


def _in_proj(xb, wb):
    T, K = xb.shape
    N = wb.shape[1]
    tm, tn = 1024, 768
    return pl.pallas_call(
        _mm_kernel,
        grid=(T // tm, N // tn),
        in_specs=[pl.BlockSpec((tm, K), lambda i, j: (i, 0)),
                  pl.BlockSpec((K, tn), lambda i, j: (0, j))],
        out_specs=pl.BlockSpec((tm, tn), lambda i, j: (i, j)),
        out_shape=jax.ShapeDtypeStruct((T, N), F32),
        compiler_params=_params(("parallel", "arbitrary")),
        name="in_proj",
    )(xb, wb)


def _ret_kernel(q_ref, k_ref, v_ref, g_ref, cos_ref, sin_ref, intra_ref, kdec_ref, qdec_ref,
                cdec_ref, gain_ref, o_ref, state_ref):
    S = q_ref.shape[0]
    C = RET_CHUNK
    scale = HEAD ** -0.5
    state_ref[...] = jnp.zeros_like(state_ref)

    def chunk(n, carry):
        rows = pl.ds(pl.multiple_of(n * C, C), C)
        cos = cos_ref[rows, :]
        sin = sin_ref[rows, :]
        q = q_ref[rows, :]
        k = k_ref[rows, :]
        qr = q * cos + pltpu.roll(q, HEAD // 2, 1) * sin
        kr = (k * cos + pltpu.roll(k, HEAD // 2, 1) * sin) * scale
        v = v_ref[rows, :].astype(BF16)
        scores = _dot_nt(qr.astype(BF16), kr.astype(BF16)) * intra_ref[...]
        state = state_ref[...]
        o = _dot(scores.astype(BF16), v) + _dot((qr * qdec_ref[...]).astype(BF16), state.astype(BF16))
        state_ref[...] = state * cdec_ref[...] + _dot_tn((kr * kdec_ref[...]).astype(BF16), v)
        g = g_ref[rows, :]
        o_ref[rows, :] = (_silu(g) * (_head_norm(o) * gain_ref[...])).astype(o_ref.dtype)
        return carry

    lax.fori_loop(0, S // C, chunk, 0, unroll=8)


def _retention(proj, cosf, sinf, tables, gain):
    B, S, _ = proj.shape
    H, C = RET_HEADS, RET_CHUNK
    intra, kdec, qdec, cdec = tables

    def col(off):
        return pl.BlockSpec((None, S, HEAD), lambda b, h: (b, 0, off + h))

    full = pl.BlockSpec((S, HEAD), lambda b, h: (0, 0))
    per_head = lambda r: pl.BlockSpec((None, r, HEAD), lambda b, h: (h, 0, 0))
    return pl.pallas_call(
        _ret_kernel,
        grid=(B, H),
        in_specs=[col(OFF_RQ), col(OFF_RK), col(OFF_RV), col(OFF_RG), full, full,
                  pl.BlockSpec((None, C, C), lambda b, h: (h, 0, 0)),
                  per_head(C), per_head(C), per_head(1), per_head(1)],
        out_specs=pl.BlockSpec((None, S, HEAD), lambda b, h: (b, 0, h)),
        out_shape=jax.ShapeDtypeStruct((B, S, H * HEAD), BF16),
        scratch_shapes=[pltpu.VMEM((HEAD, HEAD), F32)],
        compiler_params=_params(("parallel", "parallel")),
        name="retention",
    )(proj, proj, proj, proj, cosf, sinf, intra, kdec, qdec, cdec, gain)


def _dil_kernel(q_ref, k_ref, v_ref, o_ref, acc, m, s):
    S = q_ref.shape[0]
    Q = DIL_BLOCK
    scale = HEAD ** -0.5
    neg = jnp.finfo(F32).min
    ii = lax.broadcasted_iota(jnp.int32, (Q, 2 * Q), 0)
    jj = lax.broadcasted_iota(jnp.int32, (Q, 2 * Q), 1)
    cur_ok = jnp.logical_and(jj >= Q, jj - Q <= ii)
    prev_ok = jnp.logical_and(jj < Q, jj >= ii)
    order = tuple(reversed(DIL_DILATIONS))

    for pi, d in enumerate(order):
        nb = S // (d * Q)
        shift = nb.bit_length() - 1
        first, last = pi == 0, pi == len(order) - 1

        def load(ref, start, d=d):
            if d == 1:
                return ref[pl.ds(pl.multiple_of(start, Q), Q), :]
            return ref.at[pl.ds(start, (Q - 1) * d + 1), :][pl.ds(0, Q, stride=d), :]

        def store(ref, start, val, d=d):
            if d == 1:
                ref[pl.ds(pl.multiple_of(start, Q), Q), :] = val
            else:
                ref.at[pl.ds(start, (Q - 1) * d + 1), :][pl.ds(0, Q, stride=d), :] = val

        def pair(jt, carry, d=d, nb=nb, shift=shift, first=first, last=last, load=load, store=store):
            it = 2 * jt
            n0 = it & (nb - 1)
            r = it >> shift
            start = r + d * Q * n0
            has_prev = n0 > 0
            starts = (jnp.maximum(start - d * Q, r), start, start + d * Q)
            ks = [load(k_ref, st).astype(BF16) for st in starts]
            vs = [load(v_ref, st).astype(BF16) for st in starts]
            for b in range(2):
                qs = starts[b + 1]
                q = load(q_ref, qs).astype(BF16)
                sc = _dot_nt(q, jnp.concatenate([ks[b], ks[b + 1]], axis=0)) * scale
                ok = jnp.logical_or(cur_ok, jnp.logical_and(prev_ok, has_prev)) if b == 0 else jnp.logical_or(cur_ok, prev_ok)
                sc = jnp.where(ok, sc, neg)
                mx = jnp.max(sc, axis=-1, keepdims=True)
                p = jnp.exp(sc - mx)
                a = _dot(p.astype(BF16), jnp.concatenate([vs[b], vs[b + 1]], axis=0))
                m_new = jnp.broadcast_to(mx, (Q, HEAD))
                s_new = jnp.broadcast_to(jnp.sum(p, axis=-1, keepdims=True), (Q, HEAD))
                if first:
                    store(acc, qs, a)
                    store(m, qs, m_new)
                    store(s, qs, s_new)
                    continue
                m_old = load(m, qs)
                m_all = jnp.maximum(m_old, m_new)
                e_old = jnp.exp(m_old - m_all)
                e_new = jnp.exp(m_new - m_all)
                a = load(acc, qs) * e_old + a * e_new
                s_all = load(s, qs) * e_old + s_new * e_new
                if last:
                    store(o_ref, qs, (a / s_all).astype(o_ref.dtype))
                else:
                    store(acc, qs, a)
                    store(m, qs, m_all)
                    store(s, qs, s_all)
            return carry

        lax.fori_loop(0, S // (2 * Q), pair, 0, unroll=8)


def _dilated(proj):
    B, S, _ = proj.shape
    H = DIL_HEADS

    def col(off):
        return pl.BlockSpec((None, S, HEAD), lambda b, h: (b, 0, off + h))

    return pl.pallas_call(
        _dil_kernel,
        grid=(B, H),
        in_specs=[col(OFF_DQ), col(OFF_DK), col(OFF_DV)],
        out_specs=pl.BlockSpec((None, S, HEAD), lambda b, h: (b, 0, h)),
        out_shape=jax.ShapeDtypeStruct((B, S, H * HEAD), BF16),
        scratch_shapes=[pltpu.VMEM((S, HEAD), F32)] * 3,
        compiler_params=_params(("parallel", "parallel")),
        name="dilated",
    )(proj, proj, proj)


_GLA_GATE_ROWS = 512


def _gla_kernel(q_ref, k_ref, v_ref, r_ref, ga_ref, wg_ref, bg_ref, gain_ref, o_ref, la_ref, st_ref):
    S = q_ref.shape[0]
    C = GLA_CHUNK
    for c in range(S // _GLA_GATE_ROWS):
        rows = pl.ds(c * _GLA_GATE_ROWS, _GLA_GATE_ROWS)
        z = _dot(ga_ref[rows, :].astype(BF16), wg_ref[...]) + bg_ref[...]
        la_ref[rows, :] = (jnp.minimum(z, 0.0) - jnp.log1p(jnp.exp(-jnp.abs(z)))) * (1.0 / GLA_TAU)
    st_ref[...] = jnp.zeros_like(st_ref)
    row_i = lax.broadcasted_iota(jnp.int32, (C, LANE), 0)
    lane = lax.broadcasted_iota(jnp.int32, (C, LANE), 1)
    tril = lax.broadcasted_iota(jnp.int32, (C, C), 1) <= lax.broadcasted_iota(jnp.int32, (C, C), 0)

    def chunk(n, carry):
        rows = pl.ds(pl.multiple_of(n * C, C), C)
        b = la_ref[rows, :]
        sh = 1
        while sh < C:
            b = b + jnp.where(row_i >= sh, pltpu.roll(b, sh, 0), 0.0)
            sh *= 2
        q = q_ref[rows, :] * (GLA_DK ** -0.5)
        k = k_ref[rows, :]
        b_last = b[C - 1:C, :]
        qe = q * jnp.exp(b)
        ke = (k * jnp.exp(-b)).astype(BF16)
        kd = k * jnp.exp(b_last - b)
        decay = jnp.exp(b_last)
        for hh in range(2):
            mine = jnp.logical_and(lane >= GLA_DK * hh, lane < GLA_DK * (hh + 1))
            cols = pl.ds(HEAD * hh, HEAD)
            qh = jnp.where(mine, qe, 0.0).astype(BF16)
            kh = jnp.where(mine, kd, 0.0).astype(BF16)
            vh = v_ref[rows, cols].astype(BF16)
            a = jnp.where(tril, _dot_nt(qh, ke), 0.0)
            st = st_ref[hh]
            o = _dot(a.astype(BF16), vh) + _dot_nt(qh, st.astype(BF16))
            st_ref[hh] = st * decay + _dot_tn(vh, kh)
            g = r_ref[rows, cols]
            o_ref[rows, cols] = (_silu(g) * (_head_norm(o) * gain_ref[:, cols])).astype(o_ref.dtype)
        return carry

    lax.fori_loop(0, S // C, chunk, 0, unroll=8)


def _gla(proj, wg, bg, gain):
    B, S, _ = proj.shape
    P = GLA_HEADS // 2
    one = lambda off: pl.BlockSpec((None, S, LANE), lambda b, p: (b, 0, off + p))
    two = lambda off: pl.BlockSpec((None, S, 2 * LANE), lambda b, p: (b, 0, off + p))
    return pl.pallas_call(
        _gla_kernel,
        grid=(B, P),
        in_specs=[one(OFF_GQ), one(OFF_GK), two(OFF_GV), two(OFF_GR),
                  pl.BlockSpec((None, S, LANE), lambda b, p: (b, 0, OFF_GA)),
                  pl.BlockSpec((None, LANE, LANE), lambda b, p: (p, 0, 0)),
                  pl.BlockSpec((None, 1, LANE), lambda b, p: (p, 0, 0)),
                  pl.BlockSpec((None, 1, 2 * LANE), lambda b, p: (p, 0, 0))],
        out_specs=pl.BlockSpec((None, S, 2 * LANE), lambda b, p: (b, 0, p)),
        out_shape=jax.ShapeDtypeStruct((B, S, GLA_HEADS * HEAD), BF16),
        scratch_shapes=[pltpu.VMEM((S, LANE), F32), pltpu.VMEM((2, HEAD, LANE), F32)],
        compiler_params=_params(("parallel", "parallel")),
        name="gla",
    )(proj, proj, proj, proj, proj, wg, bg, gain)


def _layer_norm(y, g, b):
    mu = jnp.mean(y, axis=-1, keepdims=True)
    d = y - mu
    var = jnp.mean(d * d, axis=-1, keepdims=True)
    return d * lax.rsqrt(var + LN_EPS) * g + b


SLAB = 16


def _to_slab(ref, val):
    n = val.shape[0]
    for j in range(SLAB):
        ref[pl.ds(j, n, stride=SLAB), :] = val[:, j * LANE:(j + 1) * LANE]


def _from_slab(ref, n):
    return jnp.concatenate([ref[pl.ds(j, n, stride=SLAB), :] for j in range(SLAB)], axis=1)


def _post_kernel(ret_ref, dil_ref, gla_ref, x_ref, w_ref, g_ref, b_ref, wr_ref, br_ref, x1_ref, rt_ref):
    n_ret = ret_ref.shape[1]
    n_dil = dil_ref.shape[1]
    mixed = (_dot(ret_ref[...], w_ref[0:n_ret, :])
             + _dot(dil_ref[...], w_ref[n_ret:n_ret + n_dil, :])
             + _dot(gla_ref[...], w_ref[n_ret + n_dil:, :]))
    x1 = _layer_norm(ALPHA * x_ref[...] + mixed, g_ref[...], b_ref[...])
    _to_slab(x1_ref, x1)

    logits = _dot(x1.astype(BF16), wr_ref[...]) + br_ref[...]
    lane_i = lax.broadcasted_iota(jnp.int32, logits.shape, 1)
    lane = lane_i.astype(F32)
    far = float(LANE)
    ninf = -jnp.inf
    gmask = lane_i < N_GROUPS
    gmax = jnp.max(jnp.where(gmask, logits, ninf), axis=-1, keepdims=True)
    gsel = jnp.min(jnp.where(jnp.logical_and(gmask, logits == gmax), lane, far), axis=-1, keepdims=True)
    g_w = 1.0 / jnp.sum(jnp.where(gmask, jnp.exp(logits - gmax), 0.0), axis=-1, keepdims=True)
    lo = N_GROUPS + EXPERTS_PER_GROUP * gsel
    emask = jnp.logical_and(lane >= lo, lane < lo + EXPERTS_PER_GROUP)
    v1 = jnp.max(jnp.where(emask, logits, ninf), axis=-1, keepdims=True)
    i1 = jnp.min(jnp.where(jnp.logical_and(emask, logits == v1), lane, far), axis=-1, keepdims=True)
    emask2 = jnp.logical_and(emask, lane != i1)
    v2 = jnp.max(jnp.where(emask2, logits, ninf), axis=-1, keepdims=True)
    i2 = jnp.min(jnp.where(jnp.logical_and(emask2, logits == v2), lane, far), axis=-1, keepdims=True)
    t = jnp.exp(v2 - v1)
    gate1 = g_w / (1.0 + t)
    gate2 = g_w * t / (1.0 + t)
    rt_ref[...] = jnp.where(lane_i == 0, i1 - N_GROUPS,
                            jnp.where(lane_i == 1, i2 - N_GROUPS,
                                      jnp.where(lane_i == 2, gate1,
                                                jnp.where(lane_i == 3, gate2, 0.0))))


def _post_attn(ret, dil, gla, x, w, g, b, wr, br):
    T, D = x.shape
    tm = 256
    row = lambda n: pl.BlockSpec((tm, n), lambda i: (i, 0))
    fix = lambda shape: pl.BlockSpec(shape, lambda i: (0, 0))
    return pl.pallas_call(
        _post_kernel,
        grid=(T // tm,),
        in_specs=[row(ret.shape[1]), row(dil.shape[1]), row(gla.shape[1]), row(D),
                  fix(w.shape), fix((1, D)), fix((1, D)), fix(wr.shape), fix((1, LANE))],
        out_specs=[pl.BlockSpec((tm * SLAB, LANE), lambda i: (i, 0)), row(LANE)],
        out_shape=[jax.ShapeDtypeStruct((T * SLAB, LANE), F32), jax.ShapeDtypeStruct((T, LANE), F32)],
        compiler_params=_params(("parallel",)),
        name="post_attn",
    )(ret, dil, gla, x, w, g, b, wr, br)


def _slab_copy(src_hbm, row, dst, r, sem):
    return pltpu.make_async_copy(src_hbm.at[pl.ds(pl.multiple_of(row * SLAB, SLAB), SLAB), :],
                                 dst.at[pl.ds(r * SLAB, SLAB), :], sem)


def _rank_kernel(rt_ref, rank_ref, count_ref, offset_ref):
    i = pl.program_id(0)
    tm = rt_ref.shape[0]

    @pl.when(i == 0)
    def _():
        offset_ref[...] = jnp.zeros_like(offset_ref)

    rt = rt_ref[...]
    lane_i = lax.broadcasted_iota(jnp.int32, (tm, LANE), 1)
    lane = lane_i.astype(F32)
    hot = [(lane == rt[:, k:k + 1]).astype(F32) for k in range(TOP_K)]
    both = hot[0] + hot[1]
    before = (lax.broadcasted_iota(jnp.int32, (tm, tm), 1) < lax.broadcasted_iota(jnp.int32, (tm, tm), 0))
    seen = _dot(before.astype(BF16), both.astype(BF16)) + offset_ref[...]
    ranks = [jnp.sum(hot[k] * seen, axis=-1, keepdims=True) for k in range(TOP_K)]
    rank_ref[...] = jnp.where(lane_i == 0, ranks[0], jnp.where(lane_i == 1, ranks[1], 0.0))
    offset_ref[...] += jnp.sum(both, axis=0, keepdims=True)
    count_ref[...] = offset_ref[...]


def _rank(rt):
    T = rt.shape[0]
    tm = 512
    return pl.pallas_call(
        _rank_kernel,
        grid=(T // tm,),
        in_specs=[pl.BlockSpec((tm, LANE), lambda i: (i, 0))],
        out_specs=[pl.BlockSpec((tm, LANE), lambda i: (i, 0)), pl.BlockSpec((1, LANE), lambda i: (0, 0))],
        out_shape=[jax.ShapeDtypeStruct((T, LANE), F32), jax.ShapeDtypeStruct((1, LANE), F32)],
        scratch_shapes=[pltpu.VMEM((1, LANE), F32)],
        compiler_params=_params(("arbitrary",)),
        name="rank",
    )(rt)


def _scatter_kernel(pos_ref, x_ref, xs_in, xs_hbm, sem):
    del xs_in
    tm = x_ref.shape[0] // SLAB

    def copies():
        for r in range(tm):
            for k in range(TOP_K):
                row = pl.multiple_of(pos_ref[0, TOP_K * r + k] * SLAB, SLAB)
                yield pltpu.make_async_copy(x_ref.at[pl.ds(r * SLAB, SLAB), :],
                                            xs_hbm.at[pl.ds(row, SLAB), :], sem.at[0])

    for c in copies():
        c.start()
    for c in copies():
        c.wait()


def _scatter_rows(x1, pos, xs):
    T = pos.shape[0]
    tm = 256
    nt = T // tm
    return pl.pallas_call(
        _scatter_kernel,
        grid=(nt,),
        in_specs=[pl.BlockSpec((None, 1, TOP_K * tm), lambda i: (i, 0, 0), memory_space=pltpu.SMEM),
                  pl.BlockSpec((tm * SLAB, LANE), lambda i: (i, 0)),
                  pl.BlockSpec(memory_space=pl.ANY)],
        out_specs=pl.BlockSpec(memory_space=pl.ANY),
        out_shape=jax.ShapeDtypeStruct(xs.shape, xs.dtype),
        scratch_shapes=[pltpu.SemaphoreType.DMA((1,))],
        input_output_aliases={2: 0},
        compiler_params=_params(("arbitrary",)),
        name="scatter_rows",
    )(pos.reshape(nt, 1, TOP_K * tm), x1, xs)


def _expert_kernel(be_ref, xs_ref, wg_ref, wu_ref, wd_ref, ys_ref, wgb, wub, wdb):
    i = pl.program_id(0)
    bm = xs_ref.shape[0] // SLAB

    @pl.when(jnp.logical_or(i == 0, be_ref[i] != be_ref[jnp.maximum(i - 1, 0)]))
    def _():
        wgb[...] = wg_ref[...].astype(BF16)
        wub[...] = wu_ref[...].astype(BF16)
        wdb[...] = wd_ref[...].astype(BF16)

    xb = _from_slab(xs_ref, bm).astype(BF16)
    h = _silu(_dot(xb, wgb[...])) * _dot(xb, wub[...])
    _to_slab(ys_ref, _dot(h.astype(BF16), wdb[...]))


def _experts(xs, blk_expert, wg, wu, wd, layer):
    D = SLAB * LANE
    F = wg.shape[3]
    nblk = blk_expert.shape[0]
    bm = MOE_BM
    grid_spec = pltpu.PrefetchScalarGridSpec(
        num_scalar_prefetch=1,
        grid=(nblk,),
        in_specs=[
            pl.BlockSpec((bm * SLAB, LANE), lambda i, be: (i, 0)),
            pl.BlockSpec((None, None, D, F), lambda i, be: (layer, be[i], 0, 0)),
            pl.BlockSpec((None, None, D, F), lambda i, be: (layer, be[i], 0, 0)),
            pl.BlockSpec((None, None, F, D), lambda i, be: (layer, be[i], 0, 0)),
        ],
        out_specs=pl.BlockSpec((bm * SLAB, LANE), lambda i, be: (i, 0)),
        scratch_shapes=[pltpu.VMEM((D, F), BF16), pltpu.VMEM((D, F), BF16), pltpu.VMEM((F, D), BF16)],
    )
    return pl.pallas_call(
        _expert_kernel,
        grid_spec=grid_spec,
        out_shape=jax.ShapeDtypeStruct((nblk * bm * SLAB, LANE), F32),
        compiler_params=_params(("arbitrary",)),
        name="experts",
    )(blk_expert, xs, wg, wu, wd)


def _combine_kernel(pos_ref, posn_ref, x1_ref, rt_ref, ys_hbm, g_ref, b_ref, x2_ref, x2b_ref, buf, sem):
    i = pl.program_id(0)
    n = pl.num_programs(0)
    slot = i % 2
    tm = rt_ref.shape[0]

    def gather(pos, sl):
        for r in range(tm):
            for k in range(TOP_K):
                _slab_copy(ys_hbm, pos[0, TOP_K * r + k], buf.at[sl, k], r, sem.at[sl]).start()

    def wait(sl):
        for r in range(tm):
            for k in range(TOP_K):
                _slab_copy(ys_hbm, 0, buf.at[sl, k], r, sem.at[sl]).wait()

    @pl.when(i == 0)
    def _():
        gather(pos_ref, 0)

    wait(slot)
    rt = rt_ref[...]
    moe = rt[:, 2:3] * _from_slab(buf.at[slot, 0], tm) + rt[:, 3:4] * _from_slab(buf.at[slot, 1], tm)
    gather(posn_ref, 1 - slot)

    x2 = _layer_norm(ALPHA * _from_slab(x1_ref, tm) + moe, g_ref[...], b_ref[...])
    x2_ref[...] = x2
    x2b_ref[...] = x2.astype(BF16)

    @pl.when(i == n - 1)
    def _():
        wait(1 - slot)


def _combine(x1, rt, pos, ys, g, b):
    T = rt.shape[0]
    D = SLAB * LANE
    tm = 128
    nt = T // tm
    posr = pos.reshape(nt, 1, TOP_K * tm)
    row = lambda n: pl.BlockSpec((tm, n), lambda i: (i, 0))
    fix = lambda shape: pl.BlockSpec(shape, lambda i: (0, 0))
    return pl.pallas_call(
        _combine_kernel,
        grid=(nt,),
        in_specs=[
            pl.BlockSpec((None, 1, TOP_K * tm), lambda i: (i, 0, 0), memory_space=pltpu.SMEM),
            pl.BlockSpec((None, 1, TOP_K * tm), lambda i: (jnp.minimum(i + 1, nt - 1), 0, 0),
                         memory_space=pltpu.SMEM),
            pl.BlockSpec((tm * SLAB, LANE), lambda i: (i, 0)), row(LANE),
            pl.BlockSpec(memory_space=pl.ANY), fix((1, D)), fix((1, D)),
        ],
        out_specs=[row(D), row(D)],
        out_shape=[jax.ShapeDtypeStruct((T, D), F32), jax.ShapeDtypeStruct((T, D), BF16)],
        scratch_shapes=[pltpu.VMEM((2, TOP_K, tm * SLAB, LANE), F32), pltpu.SemaphoreType.DMA((2,))],
        compiler_params=_params(("arbitrary",)),
        name="combine",
    )(posr, posr, x1, rt, ys, g, b)


def _rope_tables(S):
    half = HEAD // 2
    inv = ROPE_BASE ** (-jnp.arange(half, dtype=F32) / half)
    ang = jnp.arange(S).astype(F32)[:, None] * inv[None, :]
    cos, sin = jnp.cos(ang), jnp.sin(ang)
    return jnp.concatenate([cos, cos], axis=-1), jnp.concatenate([-sin, sin], axis=-1)


def _ret_tables():
    H, C = RET_HEADS, RET_CHUNK
    log_g = jnp.log1p(-jnp.exp2(-5.0 - jnp.arange(H, dtype=F32)))
    idx = jnp.arange(C, dtype=F32)
    diff = idx[:, None] - idx[None, :]
    intra = jnp.where(diff >= 0, jnp.exp(jnp.maximum(diff, 0.0)[None] * log_g[:, None, None]), 0.0)
    kdec = jnp.exp((C - 1 - idx)[None] * log_g[:, None])
    qdec = jnp.exp((idx + 1)[None] * log_g[:, None])
    cdec = jnp.exp(C * log_g)
    wide = lambda t: jnp.broadcast_to(t[:, :, None], t.shape + (HEAD,))
    return intra, wide(kdec), wide(qdec), wide(cdec[:, None])


def _moe_blocks(n_tok):
    return n_tok * TOP_K // MOE_BM + N_EXPERTS


def _dispatch(rt, rank, counts):
    bm = MOE_BM
    nblk = _moe_blocks(rt.shape[0])
    counts = counts[0, :N_EXPERTS].astype(jnp.int32)
    pcounts = (counts + bm - 1) // bm * bm
    pends = jnp.cumsum(pcounts)
    pstarts = pends - pcounts
    ids = rt[:, :TOP_K].astype(jnp.int32)
    hot = ids[:, :, None] == jnp.arange(N_EXPERTS, dtype=jnp.int32)[None, None, :]
    pos = jnp.sum(jnp.where(hot, pstarts[None, None, :], 0), axis=-1) + rank[:, :TOP_K].astype(jnp.int32)
    blk_expert = jnp.minimum(
        jnp.searchsorted(pends, jnp.arange(nblk, dtype=jnp.int32) * bm, side="right"), N_EXPERTS - 1
    ).astype(jnp.int32)
    return pos, blk_expert


def kernel(x, w_in, w_gla_gate, b_gla_gate, ret_norm_g, gla_norm_g, w_out, ln1_g, ln1_b,
           w_router_group, b_router_group, w_router_expert, b_router_expert,
           w_expert_gate, w_expert_up, w_expert_down, ln2_g, ln2_b):
    B, S, D = x.shape
    T = B * S
    assert w_in.shape[2] == PROJ_REAL and S % (DIL_BLOCK * DIL_DILATIONS[-1]) == 0
    cosf, sinf = _rope_tables(S)
    tables = _ret_tables()
    gate_rank = w_gla_gate.shape[1]
    pairs = GLA_HEADS // 2
    xs = jnp.zeros((_moe_blocks(T) * MOE_BM * SLAB, LANE), F32)

    xf = x.reshape(T, D)
    xb = xf.astype(BF16)
    for l in range(DEPTH):
        w_in_p = jnp.pad(w_in[l].astype(BF16), ((0, 0), (0, PROJ_PAD - PROJ_REAL)))
        proj = _in_proj(xb, w_in_p).reshape(B, S, PROJ_PAD)

        ret = _retention(proj, cosf, sinf, tables, ret_norm_g[l].reshape(RET_HEADS, 1, HEAD))
        dil = _dilated(proj)
        wg = w_gla_gate[l].reshape(gate_rank, pairs, LANE).transpose(1, 0, 2)
        wg = jnp.pad(wg.astype(BF16), ((0, 0), (0, LANE - gate_rank), (0, 0)))
        gla = _gla(proj, wg, b_gla_gate[l].reshape(pairs, 1, LANE),
                   gla_norm_g[l].reshape(pairs, 1, 2 * HEAD))

        wr = jnp.concatenate([w_router_group[l], w_router_expert[l]], axis=1)
        wr = jnp.pad(wr.astype(BF16), ((0, 0), (0, LANE - wr.shape[1])))
        br = jnp.concatenate([b_router_group[l], b_router_expert[l]])
        br = jnp.pad(br, (0, LANE - br.shape[0])).reshape(1, LANE)
        x1, rt = _post_attn(ret.reshape(T, -1), dil.reshape(T, -1), gla.reshape(T, -1), xf,
                            w_out[l].astype(BF16), ln1_g[l].reshape(1, D), ln1_b[l].reshape(1, D), wr, br)

        rank, counts = _rank(rt)
        pos, blk_expert = _dispatch(rt, rank, counts)
        xs = _scatter_rows(x1, pos, xs)
        ys = _experts(xs, blk_expert, w_expert_gate, w_expert_up, w_expert_down, l)
        xf, xb = _combine(x1, rt, pos, ys, ln2_g[l].reshape(1, D), ln2_b[l].reshape(1, D))
    return xf.reshape(B, S, D)
```

```python
import functools

import jax
import jax.numpy as jnp
from jax import lax
from jax.experimental import pallas as pl
from jax.experimental.pallas import tpu as pltpu

F32 = jnp.float32
BF16 = jnp.bfloat16

DEPTH = 4
RET_HEADS = 4
RET_CHUNK = 128
ROPE_BASE = 10000.0
DIL_HEADS = 6
DIL_DILATIONS = (1, 4, 16)
DIL_BLOCK = 128
GLA_HEADS = 6
GLA_DK = 64
GLA_TAU = 16.0
GLA_CHUNK = 64
N_GROUPS = 4
EXPERTS_PER_GROUP = 8
N_EXPERTS = N_GROUPS * EXPERTS_PER_GROUP
TOP_K = 2
LN_EPS = 1e-5
ALPHA = (2 * DEPTH) ** 0.25

LANE = 128
HEAD = 128
OFF_RQ, OFF_RK, OFF_RV, OFF_RG = 0, 4, 8, 12
OFF_DQ, OFF_DK, OFF_DV = 16, 22, 28
OFF_GQ, OFF_GK = 34, 37
OFF_GV, OFF_GR = 20, 23
OFF_GA = 52
PROJ_REAL = 6672
PROJ_PAD = 6912
MOE_BM = 256
VMEM_LIMIT = 48 * 1024 * 1024


def _dot(a, b):
    return jnp.dot(a, b, preferred_element_type=F32)


def _dot_nt(a, b):
    return lax.dot_general(a, b, (((1,), (1,)), ((), ())), preferred_element_type=F32)


def _dot_tn(a, b):
    return lax.dot_general(a, b, (((0,), (0,)), ((), ())), preferred_element_type=F32)


def _params(sem):
    return pltpu.CompilerParams(dimension_semantics=sem, vmem_limit_bytes=VMEM_LIMIT)


def _silu(g):
    return g / (1.0 + jnp.exp(-g))


def _head_norm(o):
    mu = jnp.mean(o, axis=-1, keepdims=True)
    d = o - mu
    var = jnp.mean(d * d, axis=-1, keepdims=True)
    return d * lax.rsqrt(var + LN_EPS)


def _mm_kernel(x_ref, w_ref, o_ref):
    o_ref[...] = _dot(x_ref[...], w_ref[...])


def _in_proj(xb, wb):
    T, K = xb.shape
    N = wb.shape[1]
    tm, tn = 1024, 768
    return pl.pallas_call(
        _mm_kernel,
        grid=(T // tm, N // tn),
        in_specs=[pl.BlockSpec((tm, K), lambda i, j: (i, 0)),
                  pl.BlockSpec((K, tn), lambda i, j: (0, j))],
        out_specs=pl.BlockSpec((tm, tn), lambda i, j: (i, j)),
        out_shape=jax.ShapeDtypeStruct((T, N), F32),
        compiler_params=_params(("parallel", "arbitrary")),
        name="in_proj",
    )(xb, wb)


def _ret_kernel(q_ref, k_ref, v_ref, g_ref, cos_ref, sin_ref, intra_ref, kdec_ref, qdec_ref,
                cdec_ref, gain_ref, o_ref, state_ref):
    S = q_ref.shape[0]
    C = RET_CHUNK
    scale = HEAD ** -0.5
    state_ref[...] = jnp.zeros_like(state_ref)

    def chunk(n, carry):
        rows = pl.ds(pl.multiple_of(n * C, C), C)
        cos = cos_ref[rows, :]
        sin = sin_ref[rows, :]
        q = q_ref[rows, :]
        k = k_ref[rows, :]
        qr = q * cos + pltpu.roll(q, HEAD // 2, 1) * sin
        kr = (k * cos + pltpu.roll(k, HEAD // 2, 1) * sin) * scale
        v = v_ref[rows, :].astype(BF16)
        scores = _dot_nt(qr.astype(BF16), kr.astype(BF16)) * intra_ref[...]
        state = state_ref[...]
        o = _dot(scores.astype(BF16), v) + _dot((qr * qdec_ref[...]).astype(BF16), state.astype(BF16))
        state_ref[...] = state * cdec_ref[...] + _dot_tn((kr * kdec_ref[...]).astype(BF16), v)
        g = g_ref[rows, :]
        o_ref[rows, :] = (_silu(g) * (_head_norm(o) * gain_ref[...])).astype(o_ref.dtype)
        return carry

    lax.fori_loop(0, S // C, chunk, 0, unroll=8)


def _retention(proj, cosf, sinf, tables, gain):
    B, S, _ = proj.shape
    H, C = RET_HEADS, RET_CHUNK
    intra, kdec, qdec, cdec = tables

    def col(off):
        return pl.BlockSpec((None, S, HEAD), lambda b, h: (b, 0, off + h))

    full = pl.BlockSpec((S, HEAD), lambda b, h: (0, 0))
    per_head = lambda r: pl.BlockSpec((None, r, HEAD), lambda b, h: (h, 0, 0))
    return pl.pallas_call(
        _ret_kernel,
        grid=(B, H),
        in_specs=[col(OFF_RQ), col(OFF_RK), col(OFF_RV), col(OFF_RG), full, full,
                  pl.BlockSpec((None, C, C), lambda b, h: (h, 0, 0)),
                  per_head(C), per_head(C), per_head(1), per_head(1)],
        out_specs=pl.BlockSpec((None, S, HEAD), lambda b, h: (b, 0, h)),
        out_shape=jax.ShapeDtypeStruct((B, S, H * HEAD), BF16),
        scratch_shapes=[pltpu.VMEM((HEAD, HEAD), F32)],
        compiler_params=_params(("parallel", "parallel")),
        name="retention",
    )(proj, proj, proj, proj, cosf, sinf, intra, kdec, qdec, cdec, gain)


def _dil_kernel(q_ref, k_ref, v_ref, o_ref, acc, m, s):
    S = q_ref.shape[0]
    Q = DIL_BLOCK
    scale = HEAD ** -0.5
    neg = jnp.finfo(F32).min
    ii = lax.broadcasted_iota(jnp.int32, (Q, 2 * Q), 0)
    jj = lax.broadcasted_iota(jnp.int32, (Q, 2 * Q), 1)
    cur_ok = jnp.logical_and(jj >= Q, jj - Q <= ii)
    prev_ok = jnp.logical_and(jj < Q, jj >= ii)
    order = tuple(reversed(DIL_DILATIONS))

    for pi, d in enumerate(order):
        nb = S // (d * Q)
        shift = nb.bit_length() - 1
        first, last = pi == 0, pi == len(order) - 1

        def load(ref, start, d=d):
            if d == 1:
                return ref[pl.ds(pl.multiple_of(start, Q), Q), :]
            return ref.at[pl.ds(start, (Q - 1) * d + 1), :][pl.ds(0, Q, stride=d), :]

        def store(ref, start, val, d=d):
            if d == 1:
                ref[pl.ds(pl.multiple_of(start, Q), Q), :] = val
            else:
                ref.at[pl.ds(start, (Q - 1) * d + 1), :][pl.ds(0, Q, stride=d), :] = val

        def pair(jt, carry, d=d, nb=nb, shift=shift, first=first, last=last, load=load, store=store):
            it = 2 * jt
            n0 = it & (nb - 1)
            r = it >> shift
            start = r + d * Q * n0
            has_prev = n0 > 0
            starts = (jnp.maximum(start - d * Q, r), start, start + d * Q)
            ks = [load(k_ref, st).astype(BF16) for st in starts]
            vs = [load(v_ref, st).astype(BF16) for st in starts]
            for b in range(2):
                qs = starts[b + 1]
                q = load(q_ref, qs).astype(BF16)
                sc = _dot_nt(q, jnp.concatenate([ks[b], ks[b + 1]], axis=0)) * scale
                ok = jnp.logical_or(cur_ok, jnp.logical_and(prev_ok, has_prev)) if b == 0 else jnp.logical_or(cur_ok, prev_ok)
                sc = jnp.where(ok, sc, neg)
                mx = jnp.max(sc, axis=-1, keepdims=True)
                p = jnp.exp(sc - mx)
                a = _dot(p.astype(BF16), jnp.concatenate([vs[b], vs[b + 1]], axis=0))
                m_new = jnp.broadcast_to(mx, (Q, HEAD))
                s_new = jnp.broadcast_to(jnp.sum(p, axis=-1, keepdims=True), (Q, HEAD))
                if first:
                    store(acc, qs, a)
                    store(m, qs, m_new)
                    store(s, qs, s_new)
                    continue
                m_old = load(m, qs)
                m_all = jnp.maximum(m_old, m_new)
                e_old = jnp.exp(m_old - m_all)
                e_new = jnp.exp(m_new - m_all)
                a = load(acc, qs) * e_old + a * e_new
                s_all = load(s, qs) * e_old + s_new * e_new
                if last:
                    store(o_ref, qs, (a / s_all).astype(o_ref.dtype))
                else:
                    store(acc, qs, a)
                    store(m, qs, m_all)
                    store(s, qs, s_all)
            return carry

        lax.fori_loop(0, S // (2 * Q), pair, 0, unroll=8)


def _dilated(proj):
    B, S, _ = proj.shape
    H = DIL_HEADS

    def col(off):
        return pl.BlockSpec((None, S, HEAD), lambda b, h: (b, 0, off + h))

    return pl.pallas_call(
        _dil_kernel,
        grid=(B, H),
        in_specs=[col(OFF_DQ), col(OFF_DK), col(OFF_DV)],
        out_specs=pl.BlockSpec((None, S, HEAD), lambda b, h: (b, 0, h)),
        out_shape=jax.ShapeDtypeStruct((B, S, H * HEAD), BF16),
        scratch_shapes=[pltpu.VMEM((S, HEAD), F32)] * 3,
        compiler_params=_params(("parallel", "parallel")),
        name="dilated",
    )(proj, proj, proj)


_GLA_GATE_ROWS = 512


def _gla_kernel(q_ref, k_ref, v_ref, r_ref, ga_ref, wg_ref, bg_ref, gain_ref, o_ref, la_ref, st_ref):
    S = q_ref.shape[0]
    C = GLA_CHUNK
    for c in range(S // _GLA_GATE_ROWS):
        rows = pl.ds(c * _GLA_GATE_ROWS, _GLA_GATE_ROWS)
        z = _dot(ga_ref[rows, :].astype(BF16), wg_ref[...]) + bg_ref[...]
        la_ref[rows, :] = (jnp.minimum(z, 0.0) - jnp.log1p(jnp.exp(-jnp.abs(z)))) * (1.0 / GLA_TAU)
    st_ref[...] = jnp.zeros_like(st_ref)
    row_i = lax.broadcasted_iota(jnp.int32, (C, LANE), 0)
    lane = lax.broadcasted_iota(jnp.int32, (C, LANE), 1)
    tril = lax.broadcasted_iota(jnp.int32, (C, C), 1) <= lax.broadcasted_iota(jnp.int32, (C, C), 0)

    def chunk(n, carry):
        rows = pl.ds(pl.multiple_of(n * C, C), C)
        b = la_ref[rows, :]
        sh = 1
        while sh < C:
            b = b + jnp.where(row_i >= sh, pltpu.roll(b, sh, 0), 0.0)
            sh *= 2
        q = q_ref[rows, :] * (GLA_DK ** -0.5)
        k = k_ref[rows, :]
        b_last = b[C - 1:C, :]
        qe = q * jnp.exp(b)
        ke = (k * jnp.exp(-b)).astype(BF16)
        kd = k * jnp.exp(b_last - b)
        decay = jnp.exp(b_last)
        for hh in range(2):
            mine = jnp.logical_and(lane >= GLA_DK * hh, lane < GLA_DK * (hh + 1))
            cols = pl.ds(HEAD * hh, HEAD)
            qh = jnp.where(mine, qe, 0.0).astype(BF16)
            kh = jnp.where(mine, kd, 0.0).astype(BF16)
            vh = v_ref[rows, cols].astype(BF16)
            a = jnp.where(tril, _dot_nt(qh, ke), 0.0)
            st = st_ref[hh]
            o = _dot(a.astype(BF16), vh) + _dot_nt(qh, st.astype(BF16))
            st_ref[hh] = st * decay + _dot_tn(vh, kh)
            g = r_ref[rows, cols]
            o_ref[rows, cols] = (_silu(g) * (_head_norm(o) * gain_ref[:, cols])).astype(o_ref.dtype)
        return carry

    lax.fori_loop(0, S // C, chunk, 0, unroll=8)


def _gla(proj, wg, bg, gain):
    B, S, _ = proj.shape
    P = GLA_HEADS // 2
    one = lambda off: pl.BlockSpec((None, S, LANE), lambda b, p: (b, 0, off + p))
    two = lambda off: pl.BlockSpec((None, S, 2 * LANE), lambda b, p: (b, 0, off + p))
    return pl.pallas_call(
        _gla_kernel,
        grid=(B, P),
        in_specs=[one(OFF_GQ), one(OFF_GK), two(OFF_GV), two(OFF_GR),
                  pl.BlockSpec((None, S, LANE), lambda b, p: (b, 0, OFF_GA)),
                  pl.BlockSpec((None, LANE, LANE), lambda b, p: (p, 0, 0)),
                  pl.BlockSpec((None, 1, LANE), lambda b, p: (p, 0, 0)),
                  pl.BlockSpec((None, 1, 2 * LANE), lambda b, p: (p, 0, 0))],
        out_specs=pl.BlockSpec((None, S, 2 * LANE), lambda b, p: (b, 0, p)),
        out_shape=jax.ShapeDtypeStruct((B, S, GLA_HEADS * HEAD), BF16),
        scratch_shapes=[pltpu.VMEM((S, LANE), F32), pltpu.VMEM((2, HEAD, LANE), F32)],
        compiler_params=_params(("parallel", "parallel")),
        name="gla",
    )(proj, proj, proj, proj, proj, wg, bg, gain)


def _layer_norm(y, g, b):
    mu = jnp.mean(y, axis=-1, keepdims=True)
    d = y - mu
    var = jnp.mean(d * d, axis=-1, keepdims=True)
    return d * lax.rsqrt(var + LN_EPS) * g + b


SLAB = 16


def _to_slab(ref, val):
    n = val.shape[0]
    for j in range(SLAB):
        ref[pl.ds(j, n, stride=SLAB), :] = val[:, j * LANE:(j + 1) * LANE]


def _from_slab(ref, n):
    return jnp.concatenate([ref[pl.ds(j, n, stride=SLAB), :] for j in range(SLAB)], axis=1)


_POST_SUB = 256


def _post_kernel(ret_ref, dil_ref, gla_ref, x_ref, w_ref, g_ref, b_ref, wr_ref, br_ref, x1_ref, rt_ref):
    for s in range(x_ref.shape[0] // _POST_SUB):
        rows = pl.ds(s * _POST_SUB, _POST_SUB)
        _post_rows(ret_ref.at[rows, :], dil_ref.at[rows, :], gla_ref.at[rows, :], x_ref.at[rows, :], w_ref, g_ref,
                   b_ref, wr_ref, br_ref, x1_ref.at[pl.ds(s * _POST_SUB * SLAB, _POST_SUB * SLAB), :], rt_ref.at[rows, :])


def _post_rows(ret_ref, dil_ref, gla_ref, x_ref, w_ref, g_ref, b_ref, wr_ref, br_ref, x1_ref, rt_ref):
    n_ret = ret_ref.shape[1]
    n_dil = dil_ref.shape[1]
    mixed = (_dot(ret_ref[...], w_ref[0:n_ret, :])
             + _dot(dil_ref[...], w_ref[n_ret:n_ret + n_dil, :])
             + _dot(gla_ref[...], w_ref[n_ret + n_dil:, :]))
    x1 = _layer_norm(ALPHA * x_ref[...] + mixed, g_ref[...], b_ref[...])
    _to_slab(x1_ref, x1)

    logits = _dot(x1.astype(BF16), wr_ref[...]) + br_ref[...]
    lane_i = lax.broadcasted_iota(jnp.int32, logits.shape, 1)
    lane = lane_i.astype(F32)
    far = float(LANE)
    ninf = -jnp.inf
    gmask = lane_i < N_GROUPS
    gmax = jnp.max(jnp.where(gmask, logits, ninf), axis=-1, keepdims=True)
    gsel = jnp.min(jnp.where(jnp.logical_and(gmask, logits == gmax), lane, far), axis=-1, keepdims=True)
    g_w = 1.0 / jnp.sum(jnp.where(gmask, jnp.exp(logits - gmax), 0.0), axis=-1, keepdims=True)
    lo = N_GROUPS + EXPERTS_PER_GROUP * gsel
    emask = jnp.logical_and(lane >= lo, lane < lo + EXPERTS_PER_GROUP)
    v1 = jnp.max(jnp.where(emask, logits, ninf), axis=-1, keepdims=True)
    i1 = jnp.min(jnp.where(jnp.logical_and(emask, logits == v1), lane, far), axis=-1, keepdims=True)
    emask2 = jnp.logical_and(emask, lane != i1)
    v2 = jnp.max(jnp.where(emask2, logits, ninf), axis=-1, keepdims=True)
    i2 = jnp.min(jnp.where(jnp.logical_and(emask2, logits == v2), lane, far), axis=-1, keepdims=True)
    t = jnp.exp(v2 - v1)
    gate1 = g_w / (1.0 + t)
    gate2 = g_w * t / (1.0 + t)
    rt_ref[...] = jnp.where(lane_i == 0, i1 - N_GROUPS,
                            jnp.where(lane_i == 1, i2 - N_GROUPS,
                                      jnp.where(lane_i == 2, gate1,
                                                jnp.where(lane_i == 3, gate2, 0.0))))


def _post_attn(ret, dil, gla, x, w, g, b, wr, br):
    T, D = x.shape
    tm = 2 * _POST_SUB
    row = lambda n: pl.BlockSpec((tm, n), lambda i: (i, 0))
    fix = lambda shape: pl.BlockSpec(shape, lambda i: (0, 0))
    return pl.pallas_call(
        _post_kernel,
        grid=(T // tm,),
        in_specs=[row(ret.shape[1]), row(dil.shape[1]), row(gla.shape[1]), row(D),
                  fix(w.shape), fix((1, D)), fix((1, D)), fix(wr.shape), fix((1, LANE))],
        out_specs=[pl.BlockSpec((tm * SLAB, LANE), lambda i: (i, 0)), row(LANE)],
        out_shape=[jax.ShapeDtypeStruct((T * SLAB, LANE), F32), jax.ShapeDtypeStruct((T, LANE), F32)],
        compiler_params=_params(("parallel",)),
        name="post_attn",
    )(ret, dil, gla, x, w, g, b, wr, br)


def _slab_copy(src_hbm, row, dst, r, sem):
    return pltpu.make_async_copy(src_hbm.at[pl.ds(pl.multiple_of(row * SLAB, SLAB), SLAB), :],
                                 dst.at[pl.ds(r * SLAB, SLAB), :], sem)


def _rank_kernel(rt_ref, rank_ref, count_ref, offset_ref):
    i = pl.program_id(0)
    tm = rt_ref.shape[0]

    @pl.when(i == 0)
    def _():
        offset_ref[...] = jnp.zeros_like(offset_ref)

    rt = rt_ref[...]
    lane_i = lax.broadcasted_iota(jnp.int32, (tm, LANE), 1)
    lane = lane_i.astype(F32)
    hot = [(lane == rt[:, k:k + 1]).astype(F32) for k in range(TOP_K)]
    both = hot[0] + hot[1]
    before = (lax.broadcasted_iota(jnp.int32, (tm, tm), 1) < lax.broadcasted_iota(jnp.int32, (tm, tm), 0))
    seen = _dot(before.astype(BF16), both.astype(BF16)) + offset_ref[...]
    ranks = [jnp.sum(hot[k] * seen, axis=-1, keepdims=True) for k in range(TOP_K)]
    rank_ref[...] = jnp.where(lane_i == 0, ranks[0], jnp.where(lane_i == 1, ranks[1], 0.0))
    offset_ref[...] += jnp.sum(both, axis=0, keepdims=True)
    count_ref[...] = offset_ref[...]


def _rank(rt):
    T = rt.shape[0]
    tm = 512
    return pl.pallas_call(
        _rank_kernel,
        grid=(T // tm,),
        in_specs=[pl.BlockSpec((tm, LANE), lambda i: (i, 0))],
        out_specs=[pl.BlockSpec((tm, LANE), lambda i: (i, 0)), pl.BlockSpec((1, LANE), lambda i: (0, 0))],
        out_shape=[jax.ShapeDtypeStruct((T, LANE), F32), jax.ShapeDtypeStruct((1, LANE), F32)],
        scratch_shapes=[pltpu.VMEM((1, LANE), F32)],
        compiler_params=_params(("arbitrary",)),
        name="rank",
    )(rt)


def _scatter_kernel(pos_ref, x_ref, xs_in, xs_hbm, sem):
    del xs_in
    tm = x_ref.shape[0] // SLAB

    def copies():
        for r in range(tm):
            for k in range(TOP_K):
                row = pl.multiple_of(pos_ref[0, TOP_K * r + k] * SLAB, SLAB)
                yield pltpu.make_async_copy(x_ref.at[pl.ds(r * SLAB, SLAB), :],
                                            xs_hbm.at[pl.ds(row, SLAB), :], sem.at[0])

    for c in copies():
        c.start()
    for c in copies():
        c.wait()


def _scatter_rows(x1, pos, xs):
    T = pos.shape[0]
    tm = 256
    nt = T // tm
    return pl.pallas_call(
        _scatter_kernel,
        grid=(nt,),
        in_specs=[pl.BlockSpec((None, 1, TOP_K * tm), lambda i: (i, 0, 0), memory_space=pltpu.SMEM),
                  pl.BlockSpec((tm * SLAB, LANE), lambda i: (i, 0)),
                  pl.BlockSpec(memory_space=pl.ANY)],
        out_specs=pl.BlockSpec(memory_space=pl.ANY),
        out_shape=jax.ShapeDtypeStruct(xs.shape, xs.dtype),
        scratch_shapes=[pltpu.SemaphoreType.DMA((1,))],
        input_output_aliases={2: 0},
        compiler_params=_params(("arbitrary",)),
        name="scatter_rows",
    )(pos.reshape(nt, 1, TOP_K * tm), x1, xs)


def _expert_kernel(be_ref, xs_ref, wg_ref, wu_ref, wd_ref, ys_ref, wgb, wub, wdb):
    i = pl.program_id(0)
    bm = xs_ref.shape[0] // SLAB

    @pl.when(jnp.logical_or(i == 0, be_ref[i] != be_ref[jnp.maximum(i - 1, 0)]))
    def _():
        wgb[...] = wg_ref[...].astype(BF16)
        wub[...] = wu_ref[...].astype(BF16)
        wdb[...] = wd_ref[...].astype(BF16)

    xb = _from_slab(xs_ref, bm).astype(BF16)
    h = _silu(_dot(xb, wgb[...])) * _dot(xb, wub[...])
    _to_slab(ys_ref, _dot(h.astype(BF16), wdb[...]))


def _experts(xs, blk_expert, wg, wu, wd, layer):
    D = SLAB * LANE
    F = wg.shape[3]
    nblk = blk_expert.shape[0]
    bm = MOE_BM
    grid_spec = pltpu.PrefetchScalarGridSpec(
        num_scalar_prefetch=1,
        grid=(nblk,),
        in_specs=[
            pl.BlockSpec((bm * SLAB, LANE), lambda i, be: (i, 0)),
            pl.BlockSpec((None, None, D, F), lambda i, be: (layer, be[i], 0, 0)),
            pl.BlockSpec((None, None, D, F), lambda i, be: (layer, be[i], 0, 0)),
            pl.BlockSpec((None, None, F, D), lambda i, be: (layer, be[i], 0, 0)),
        ],
        out_specs=pl.BlockSpec((bm * SLAB, LANE), lambda i, be: (i, 0)),
        scratch_shapes=[pltpu.VMEM((D, F), BF16), pltpu.VMEM((D, F), BF16), pltpu.VMEM((F, D), BF16)],
    )
    return pl.pallas_call(
        _expert_kernel,
        grid_spec=grid_spec,
        out_shape=jax.ShapeDtypeStruct((nblk * bm * SLAB, LANE), F32),
        compiler_params=_params(("arbitrary",)),
        name="experts",
    )(blk_expert, xs, wg, wu, wd)


def _combine_kernel(pos_ref, posn_ref, x1_ref, rt_ref, ys_hbm, g_ref, b_ref, x2_ref, x2b_ref, buf, sem):
    i = pl.program_id(0)
    n = pl.num_programs(0)
    slot = i % 2
    tm = rt_ref.shape[0]

    def gather(pos, sl):
        for r in range(tm):
            for k in range(TOP_K):
                _slab_copy(ys_hbm, pos[0, TOP_K * r + k], buf.at[sl, k], r, sem.at[sl]).start()

    def wait(sl):
        for r in range(tm):
            for k in range(TOP_K):
                _slab_copy(ys_hbm, 0, buf.at[sl, k], r, sem.at[sl]).wait()

    @pl.when(i == 0)
    def _():
        gather(pos_ref, 0)

    wait(slot)
    rt = rt_ref[...]
    moe = rt[:, 2:3] * _from_slab(buf.at[slot, 0], tm) + rt[:, 3:4] * _from_slab(buf.at[slot, 1], tm)
    gather(posn_ref, 1 - slot)

    x2 = _layer_norm(ALPHA * _from_slab(x1_ref, tm) + moe, g_ref[...], b_ref[...])
    x2_ref[...] = x2
    x2b_ref[...] = x2.astype(BF16)

    @pl.when(i == n - 1)
    def _():
        wait(1 - slot)


def _combine(x1, rt, pos, ys, g, b):
    T = rt.shape[0]
    D = SLAB * LANE
    tm = 128
    nt = T // tm
    posr = pos.reshape(nt, 1, TOP_K * tm)
    row = lambda n: pl.BlockSpec((tm, n), lambda i: (i, 0))
    fix = lambda shape: pl.BlockSpec(shape, lambda i: (0, 0))
    return pl.pallas_call(
        _combine_kernel,
        grid=(nt,),
        in_specs=[
            pl.BlockSpec((None, 1, TOP_K * tm), lambda i: (i, 0, 0), memory_space=pltpu.SMEM),
            pl.BlockSpec((None, 1, TOP_K * tm), lambda i: (jnp.minimum(i + 1, nt - 1), 0, 0),
                         memory_space=pltpu.SMEM),
            pl.BlockSpec((tm * SLAB, LANE), lambda i: (i, 0)), row(LANE),
            pl.BlockSpec(memory_space=pl.ANY), fix((1, D)), fix((1, D)),
        ],
        out_specs=[row(D), row(D)],
        out_shape=[jax.ShapeDtypeStruct((T, D), F32), jax.ShapeDtypeStruct((T, D), BF16)],
        scratch_shapes=[pltpu.VMEM((2, TOP_K, tm * SLAB, LANE), F32), pltpu.SemaphoreType.DMA((2,))],
        compiler_params=_params(("arbitrary",)),
        name="combine",
    )(posr, posr, x1, rt, ys, g, b)


def _rope_tables(S):
    half = HEAD // 2
    inv = ROPE_BASE ** (-jnp.arange(half, dtype=F32) / half)
    ang = jnp.arange(S).astype(F32)[:, None] * inv[None, :]
    cos, sin = jnp.cos(ang), jnp.sin(ang)
    return jnp.concatenate([cos, cos], axis=-1), jnp.concatenate([-sin, sin], axis=-1)


def _ret_tables():
    H, C = RET_HEADS, RET_CHUNK
    log_g = jnp.log1p(-jnp.exp2(-5.0 - jnp.arange(H, dtype=F32)))
    idx = jnp.arange(C, dtype=F32)
    diff = idx[:, None] - idx[None, :]
    intra = jnp.where(diff >= 0, jnp.exp(jnp.maximum(diff, 0.0)[None] * log_g[:, None, None]), 0.0)
    kdec = jnp.exp((C - 1 - idx)[None] * log_g[:, None])
    qdec = jnp.exp((idx + 1)[None] * log_g[:, None])
    cdec = jnp.exp(C * log_g)
    wide = lambda t: jnp.broadcast_to(t[:, :, None], t.shape + (HEAD,))
    return intra, wide(kdec), wide(qdec), wide(cdec[:, None])


def _moe_blocks(n_tok):
    return n_tok * TOP_K // MOE_BM + N_EXPERTS


def _dispatch(rt, rank, counts):
    bm = MOE_BM
    nblk = _moe_blocks(rt.shape[0])
    counts = counts[0, :N_EXPERTS].astype(jnp.int32)
    pcounts = (counts + bm - 1) // bm * bm
    pends = jnp.cumsum(pcounts)
    pstarts = pends - pcounts
    ids = rt[:, :TOP_K].astype(jnp.int32)
    hot = ids[:, :, None] == jnp.arange(N_EXPERTS, dtype=jnp.int32)[None, None, :]
    pos = jnp.sum(jnp.where(hot, pstarts[None, None, :], 0), axis=-1) + rank[:, :TOP_K].astype(jnp.int32)
    blk_start = jnp.arange(nblk, dtype=jnp.int32) * bm
    blk_expert = jnp.sum((pends[None, :] <= blk_start[:, None]).astype(jnp.int32), axis=1)
    return pos, jnp.minimum(blk_expert, N_EXPERTS - 1)


def kernel(x, w_in, w_gla_gate, b_gla_gate, ret_norm_g, gla_norm_g, w_out, ln1_g, ln1_b,
           w_router_group, b_router_group, w_router_expert, b_router_expert,
           w_expert_gate, w_expert_up, w_expert_down, ln2_g, ln2_b):
    B, S, D = x.shape
    T = B * S
    assert w_in.shape[2] == PROJ_REAL and S % (DIL_BLOCK * DIL_DILATIONS[-1]) == 0
    cosf, sinf = _rope_tables(S)
    tables = _ret_tables()
    gate_rank = w_gla_gate.shape[1]
    pairs = GLA_HEADS // 2
    xs = jnp.zeros((_moe_blocks(T) * MOE_BM * SLAB, LANE), F32)

    xf = x.reshape(T, D)
    xb = xf.astype(BF16)
    for l in range(DEPTH):
        w_in_p = jnp.pad(w_in[l].astype(BF16), ((0, 0), (0, PROJ_PAD - PROJ_REAL)))
        proj = _in_proj(xb, w_in_p).reshape(B, S, PROJ_PAD)

        ret = _retention(proj, cosf, sinf, tables, ret_norm_g[l].reshape(RET_HEADS, 1, HEAD))
        dil = _dilated(proj)
        wg = w_gla_gate[l].reshape(gate_rank, pairs, LANE).transpose(1, 0, 2)
        wg = jnp.pad(wg.astype(BF16), ((0, 0), (0, LANE - gate_rank), (0, 0)))
        gla = _gla(proj, wg, b_gla_gate[l].reshape(pairs, 1, LANE),
                   gla_norm_g[l].reshape(pairs, 1, 2 * HEAD))

        wr = jnp.concatenate([w_router_group[l], w_router_expert[l]], axis=1)
        wr = jnp.pad(wr.astype(BF16), ((0, 0), (0, LANE - wr.shape[1])))
        br = jnp.concatenate([b_router_group[l], b_router_expert[l]])
        br = jnp.pad(br, (0, LANE - br.shape[0])).reshape(1, LANE)
        x1, rt = _post_attn(ret.reshape(T, -1), dil.reshape(T, -1), gla.reshape(T, -1), xf,
                            w_out[l].astype(BF16), ln1_g[l].reshape(1, D), ln1_b[l].reshape(1, D), wr, br)

        rank, counts = _rank(rt)
        pos, blk_expert = _dispatch(rt, rank, counts)
        xs = _scatter_rows(x1, pos, xs)
        ys = _experts(xs, blk_expert, w_expert_gate, w_expert_up, w_expert_down, l)
        xf, xb = _combine(x1, rt, pos, ys, ln2_g[l].reshape(1, D), ln2_b[l].reshape(1, D))
    return xf.reshape(B, S, D)
```

```python
import functools

import jax
import jax.numpy as jnp
from jax import lax
from jax.experimental import pallas as pl
from jax.experimental.pallas import tpu as pltpu

F32 = jnp.float32
BF16 = jnp.bfloat16

DEPTH = 4
RET_HEADS = 4
RET_CHUNK = 128
ROPE_BASE = 10000.0
DIL_HEADS = 6
DIL_DILATIONS = (1, 4, 16)
DIL_BLOCK = 128
GLA_HEADS = 6
GLA_DK = 64
GLA_TAU = 16.0
GLA_CHUNK = 64
N_GROUPS = 4
EXPERTS_PER_GROUP = 8
N_EXPERTS = N_GROUPS * EXPERTS_PER_GROUP
TOP_K = 2
LN_EPS = 1e-5
ALPHA = (2 * DEPTH) ** 0.25

LANE = 128
HEAD = 128
OFF_RQ, OFF_RK, OFF_RV, OFF_RG = 0, 4, 8, 12
OFF_DQ, OFF_DK, OFF_DV = 16, 22, 28
OFF_GQ, OFF_GK = 34, 37
OFF_GV, OFF_GR = 20, 23
OFF_GA = 52
PROJ_REAL = 6672
PROJ_PAD = 6912
MOE_BM = 256
VMEM_LIMIT = 48 * 1024 * 1024


def _dot(a, b):
    return jnp.dot(a, b, preferred_element_type=F32)


def _dot_nt(a, b):
    return lax.dot_general(a, b, (((1,), (1,)), ((), ())), preferred_element_type=F32)


def _dot_tn(a, b):
    return lax.dot_general(a, b, (((0,), (0,)), ((), ())), preferred_element_type=F32)


def _params(sem):
    return pltpu.CompilerParams(dimension_semantics=sem, vmem_limit_bytes=VMEM_LIMIT)


def _silu(g):
    return g / (1.0 + jnp.exp(-g))


def _head_norm(o):
    mu = jnp.mean(o, axis=-1, keepdims=True)
    d = o - mu
    var = jnp.mean(d * d, axis=-1, keepdims=True)
    return d * lax.rsqrt(var + LN_EPS)


def _mm_kernel(x_ref, w_ref, o_ref):
    o_ref[...] = _dot(x_ref[...], w_ref[...])


def _in_proj(xb, wb):
    T, K = xb.shape
    N = wb.shape[1]
    tm, tn = 1024, 768
    return pl.pallas_call(
        _mm_kernel,
        grid=(T // tm, N // tn),
        in_specs=[pl.BlockSpec((tm, K), lambda i, j: (i, 0)),
                  pl.BlockSpec((K, tn), lambda i, j: (0, j))],
        out_specs=pl.BlockSpec((tm, tn), lambda i, j: (i, j)),
        out_shape=jax.ShapeDtypeStruct((T, N), F32),
        compiler_params=_params(("parallel", "arbitrary")),
        name="in_proj",
    )(xb, wb)


def _ret_kernel(q_ref, k_ref, v_ref, g_ref, cos_ref, sin_ref, intra_ref, kdec_ref, qdec_ref,
                cdec_ref, gain_ref, o_ref, state_ref):
    S = q_ref.shape[0]
    C = RET_CHUNK
    scale = HEAD ** -0.5
    state_ref[...] = jnp.zeros_like(state_ref)

    def chunk(n, carry):
        rows = pl.ds(pl.multiple_of(n * C, C), C)
        cos = cos_ref[rows, :]
        sin = sin_ref[rows, :]
        q = q_ref[rows, :]
        k = k_ref[rows, :]
        qr = q * cos + pltpu.roll(q, HEAD // 2, 1) * sin
        kr = (k * cos + pltpu.roll(k, HEAD // 2, 1) * sin) * scale
        v = v_ref[rows, :].astype(BF16)
        scores = _dot_nt(qr.astype(BF16), kr.astype(BF16)) * intra_ref[...]
        state = state_ref[...]
        o = _dot(scores.astype(BF16), v) + _dot((qr * qdec_ref[...]).astype(BF16), state.astype(BF16))
        state_ref[...] = state * cdec_ref[...] + _dot_tn((kr * kdec_ref[...]).astype(BF16), v)
        g = g_ref[rows, :]
        o_ref[rows, :] = (_silu(g) * (_head_norm(o) * gain_ref[...])).astype(o_ref.dtype)
        return carry

    lax.fori_loop(0, S // C, chunk, 0, unroll=8)


def _retention(proj, cosf, sinf, tables, gain):
    B, S, _ = proj.shape
    H, C = RET_HEADS, RET_CHUNK
    intra, kdec, qdec, cdec = tables

    def col(off):
        return pl.BlockSpec((None, S, HEAD), lambda b, h: (b, 0, off + h))

    full = pl.BlockSpec((S, HEAD), lambda b, h: (0, 0))
    per_head = lambda r: pl.BlockSpec((None, r, HEAD), lambda b, h: (h, 0, 0))
    return pl.pallas_call(
        _ret_kernel,
        grid=(B, H),
        in_specs=[col(OFF_RQ), col(OFF_RK), col(OFF_RV), col(OFF_RG), full, full,
                  pl.BlockSpec((None, C, C), lambda b, h: (h, 0, 0)),
                  per_head(C), per_head(C), per_head(1), per_head(1)],
        out_specs=pl.BlockSpec((None, S, HEAD), lambda b, h: (b, 0, h)),
        out_shape=jax.ShapeDtypeStruct((B, S, H * HEAD), BF16),
        scratch_shapes=[pltpu.VMEM((HEAD, HEAD), F32)],
        compiler_params=_params(("parallel", "parallel")),
        name="retention",
    )(proj, proj, proj, proj, cosf, sinf, intra, kdec, qdec, cdec, gain)


def _dil_kernel(q_ref, k_ref, v_ref, o_ref, acc, m, s):
    S = q_ref.shape[0]
    Q = DIL_BLOCK
    scale = HEAD ** -0.5
    neg = jnp.finfo(F32).min
    ii = lax.broadcasted_iota(jnp.int32, (Q, 2 * Q), 0)
    jj = lax.broadcasted_iota(jnp.int32, (Q, 2 * Q), 1)
    cur_ok = jnp.logical_and(jj >= Q, jj - Q <= ii)
    prev_ok = jnp.logical_and(jj < Q, jj >= ii)
    order = tuple(reversed(DIL_DILATIONS))

    for pi, d in enumerate(order):
        nb = S // (d * Q)
        shift = nb.bit_length() - 1
        first, last = pi == 0, pi == len(order) - 1

        def load(ref, start, d=d):
            if d == 1:
                return ref[pl.ds(pl.multiple_of(start, Q), Q), :]
            return ref.at[pl.ds(start, (Q - 1) * d + 1), :][pl.ds(0, Q, stride=d), :]

        def store(ref, start, val, d=d):
            if d == 1:
                ref[pl.ds(pl.multiple_of(start, Q), Q), :] = val
            else:
                ref.at[pl.ds(start, (Q - 1) * d + 1), :][pl.ds(0, Q, stride=d), :] = val

        def pair(jt, carry, d=d, nb=nb, shift=shift, first=first, last=last, load=load, store=store):
            it = 2 * jt
            n0 = it & (nb - 1)
            r = it >> shift
            start = r + d * Q * n0
            has_prev = n0 > 0
            starts = (jnp.maximum(start - d * Q, r), start, start + d * Q)
            ks = [load(k_ref, st).astype(BF16) for st in starts]
            vs = [load(v_ref, st).astype(BF16) for st in starts]
            for b in range(2):
                qs = starts[b + 1]
                q = load(q_ref, qs).astype(BF16)
                sc = _dot_nt(q, jnp.concatenate([ks[b], ks[b + 1]], axis=0)) * scale
                ok = jnp.logical_or(cur_ok, jnp.logical_and(prev_ok, has_prev)) if b == 0 else jnp.logical_or(cur_ok, prev_ok)
                sc = jnp.where(ok, sc, neg)
                mx = jnp.max(sc, axis=-1, keepdims=True)
                p = jnp.exp(sc - mx)
                a = _dot(p.astype(BF16), jnp.concatenate([vs[b], vs[b + 1]], axis=0))
                m_new = jnp.broadcast_to(mx, (Q, HEAD))
                s_new = jnp.broadcast_to(jnp.sum(p, axis=-1, keepdims=True), (Q, HEAD))
                if first:
                    store(acc, qs, a)
                    store(m, qs, m_new)
                    store(s, qs, s_new)
                    continue
                m_old = load(m, qs)
                m_all = jnp.maximum(m_old, m_new)
                e_old = jnp.exp(m_old - m_all)
                e_new = jnp.exp(m_new - m_all)
                a = load(acc, qs) * e_old + a * e_new
                s_all = load(s, qs) * e_old + s_new * e_new
                if last:
                    store(o_ref, qs, (a / s_all).astype(o_ref.dtype))
                else:
                    store(acc, qs, a)
                    store(m, qs, m_all)
                    store(s, qs, s_all)
            return carry

        lax.fori_loop(0, S // (2 * Q), pair, 0, unroll=8)


def _dilated(proj):
    B, S, _ = proj.shape
    H = DIL_HEADS

    def col(off):
        return pl.BlockSpec((None, S, HEAD), lambda b, h: (b, 0, off + h))

    return pl.pallas_call(
        _dil_kernel,
        grid=(B, H),
        in_specs=[col(OFF_DQ), col(OFF_DK), col(OFF_DV)],
        out_specs=pl.BlockSpec((None, S, HEAD), lambda b, h: (b, 0, h)),
        out_shape=jax.ShapeDtypeStruct((B, S, H * HEAD), BF16),
        scratch_shapes=[pltpu.VMEM((S, HEAD), F32)] * 3,
        compiler_params=_params(("parallel", "parallel")),
        name="dilated",
    )(proj, proj, proj)


_GLA_GATE_ROWS = 512


def _gla_kernel(q_ref, k_ref, v_ref, r_ref, ga_ref, wg_ref, bg_ref, gain_ref, o_ref, la_ref, st_ref):
    S = q_ref.shape[0]
    C = GLA_CHUNK
    for c in range(S // _GLA_GATE_ROWS):
        rows = pl.ds(c * _GLA_GATE_ROWS, _GLA_GATE_ROWS)
        z = _dot(ga_ref[rows, :].astype(BF16), wg_ref[...]) + bg_ref[...]
        la_ref[rows, :] = (jnp.minimum(z, 0.0) - jnp.log1p(jnp.exp(-jnp.abs(z)))) * (1.0 / GLA_TAU)
    st_ref[...] = jnp.zeros_like(st_ref)
    row_i = lax.broadcasted_iota(jnp.int32, (C, LANE), 0)
    lane = lax.broadcasted_iota(jnp.int32, (C, LANE), 1)
    tril = lax.broadcasted_iota(jnp.int32, (C, C), 1) <= lax.broadcasted_iota(jnp.int32, (C, C), 0)

    def chunk(n, carry):
        rows = pl.ds(pl.multiple_of(n * C, C), C)
        b = la_ref[rows, :]
        sh = 1
        while sh < C:
            b = b + jnp.where(row_i >= sh, pltpu.roll(b, sh, 0), 0.0)
            sh *= 2
        q = q_ref[rows, :] * (GLA_DK ** -0.5)
        k = k_ref[rows, :]
        b_last = b[C - 1:C, :]
        qe = q * jnp.exp(b)
        ke = (k * jnp.exp(-b)).astype(BF16)
        kd = k * jnp.exp(b_last - b)
        decay = jnp.exp(b_last)
        for hh in range(2):
            mine = jnp.logical_and(lane >= GLA_DK * hh, lane < GLA_DK * (hh + 1))
            cols = pl.ds(HEAD * hh, HEAD)
            qh = jnp.where(mine, qe, 0.0).astype(BF16)
            kh = jnp.where(mine, kd, 0.0).astype(BF16)
            vh = v_ref[rows, cols].astype(BF16)
            a = jnp.where(tril, _dot_nt(qh, ke), 0.0)
            st = st_ref[hh]
            o = _dot(a.astype(BF16), vh) + _dot_nt(qh, st.astype(BF16))
            st_ref[hh] = st * decay + _dot_tn(vh, kh)
            g = r_ref[rows, cols]
            o_ref[rows, cols] = (_silu(g) * (_head_norm(o) * gain_ref[:, cols])).astype(o_ref.dtype)
        return carry

    lax.fori_loop(0, S // C, chunk, 0, unroll=8)


def _gla(proj, wg, bg, gain):
    B, S, _ = proj.shape
    P = GLA_HEADS // 2
    one = lambda off: pl.BlockSpec((None, S, LANE), lambda b, p: (b, 0, off + p))
    two = lambda off: pl.BlockSpec((None, S, 2 * LANE), lambda b, p: (b, 0, off + p))
    return pl.pallas_call(
        _gla_kernel,
        grid=(B, P),
        in_specs=[one(OFF_GQ), one(OFF_GK), two(OFF_GV), two(OFF_GR),
                  pl.BlockSpec((None, S, LANE), lambda b, p: (b, 0, OFF_GA)),
                  pl.BlockSpec((None, LANE, LANE), lambda b, p: (p, 0, 0)),
                  pl.BlockSpec((None, 1, LANE), lambda b, p: (p, 0, 0)),
                  pl.BlockSpec((None, 1, 2 * LANE), lambda b, p: (p, 0, 0))],
        out_specs=pl.BlockSpec((None, S, 2 * LANE), lambda b, p: (b, 0, p)),
        out_shape=jax.ShapeDtypeStruct((B, S, GLA_HEADS * HEAD), BF16),
        scratch_shapes=[pltpu.VMEM((S, LANE), F32), pltpu.VMEM((2, HEAD, LANE), F32)],
        compiler_params=_params(("parallel", "parallel")),
        name="gla",
    )(proj, proj, proj, proj, proj, wg, bg, gain)


def _layer_norm(y, g, b):
    mu = jnp.mean(y, axis=-1, keepdims=True)
    d = y - mu
    var = jnp.mean(d * d, axis=-1, keepdims=True)
    return d * lax.rsqrt(var + LN_EPS) * g + b


SLAB = 16


def _to_slab(ref, val):
    n = val.shape[0]
    for j in range(SLAB):
        ref[pl.ds(j, n, stride=SLAB), :] = val[:, j * LANE:(j + 1) * LANE]


def _from_slab(ref, n):
    return jnp.concatenate([ref[pl.ds(j, n, stride=SLAB), :] for j in range(SLAB)], axis=1)


_POST_SUB = 256


def _post_kernel(ret_ref, dil_ref, gla_ref, x_ref, w_ref, g_ref, b_ref, wr_ref, br_ref, x1_ref, rt_ref):
    for s in range(x_ref.shape[0] // _POST_SUB):
        rows = pl.ds(s * _POST_SUB, _POST_SUB)
        _post_rows(ret_ref.at[rows, :], dil_ref.at[rows, :], gla_ref.at[rows, :], x_ref.at[rows, :], w_ref, g_ref,
                   b_ref, wr_ref, br_ref, x1_ref.at[pl.ds(s * _POST_SUB * SLAB, _POST_SUB * SLAB), :], rt_ref.at[rows, :])


def _post_rows(ret_ref, dil_ref, gla_ref, x_ref, w_ref, g_ref, b_ref, wr_ref, br_ref, x1_ref, rt_ref):
    n_ret = ret_ref.shape[1]
    n_dil = dil_ref.shape[1]
    mixed = (_dot(ret_ref[...], w_ref[0:n_ret, :])
             + _dot(dil_ref[...], w_ref[n_ret:n_ret + n_dil, :])
             + _dot(gla_ref[...], w_ref[n_ret + n_dil:, :]))
    x1 = _layer_norm(ALPHA * x_ref[...] + mixed, g_ref[...], b_ref[...])
    _to_slab(x1_ref, x1)

    logits = _dot(x1.astype(BF16), wr_ref[...]) + br_ref[...]
    lane_i = lax.broadcasted_iota(jnp.int32, logits.shape, 1)
    lane = lane_i.astype(F32)
    far = float(LANE)
    ninf = -jnp.inf
    gmask = lane_i < N_GROUPS
    gmax = jnp.max(jnp.where(gmask, logits, ninf), axis=-1, keepdims=True)
    gsel = jnp.min(jnp.where(jnp.logical_and(gmask, logits == gmax), lane, far), axis=-1, keepdims=True)
    g_w = 1.0 / jnp.sum(jnp.where(gmask, jnp.exp(logits - gmax), 0.0), axis=-1, keepdims=True)
    lo = N_GROUPS + EXPERTS_PER_GROUP * gsel
    emask = jnp.logical_and(lane >= lo, lane < lo + EXPERTS_PER_GROUP)
    v1 = jnp.max(jnp.where(emask, logits, ninf), axis=-1, keepdims=True)
    i1 = jnp.min(jnp.where(jnp.logical_and(emask, logits == v1), lane, far), axis=-1, keepdims=True)
    emask2 = jnp.logical_and(emask, lane != i1)
    v2 = jnp.max(jnp.where(emask2, logits, ninf), axis=-1, keepdims=True)
    i2 = jnp.min(jnp.where(jnp.logical_and(emask2, logits == v2), lane, far), axis=-1, keepdims=True)
    t = jnp.exp(v2 - v1)
    gate1 = g_w / (1.0 + t)
    gate2 = g_w * t / (1.0 + t)
    rt_ref[...] = jnp.where(lane_i == 0, i1 - N_GROUPS,
                            jnp.where(lane_i == 1, i2 - N_GROUPS,
                                      jnp.where(lane_i == 2, gate1,
                                                jnp.where(lane_i == 3, gate2, 0.0))))


def _post_attn(ret, dil, gla, x, w, g, b, wr, br):
    T, D = x.shape
    tm = 2 * _POST_SUB
    row = lambda n: pl.BlockSpec((tm, n), lambda i: (i, 0))
    fix = lambda shape: pl.BlockSpec(shape, lambda i: (0, 0))
    return pl.pallas_call(
        _post_kernel,
        grid=(T // tm,),
        in_specs=[row(ret.shape[1]), row(dil.shape[1]), row(gla.shape[1]), row(D),
                  fix(w.shape), fix((1, D)), fix((1, D)), fix(wr.shape), fix((1, LANE))],
        out_specs=[pl.BlockSpec((tm * SLAB, LANE), lambda i: (i, 0)), row(LANE)],
        out_shape=[jax.ShapeDtypeStruct((T * SLAB, LANE), F32), jax.ShapeDtypeStruct((T, LANE), F32)],
        compiler_params=_params(("parallel",)),
        name="post_attn",
    )(ret, dil, gla, x, w, g, b, wr, br)


def _slab_copy(src_hbm, row, dst, r, sem):
    return pltpu.make_async_copy(src_hbm.at[pl.ds(pl.multiple_of(row * SLAB, SLAB), SLAB), :],
                                 dst.at[pl.ds(r * SLAB, SLAB), :], sem)


def _rank_kernel(rt_ref, rank_ref, count_ref, offset_ref):
    i = pl.program_id(0)
    tm = rt_ref.shape[0]

    @pl.when(i == 0)
    def _():
        offset_ref[...] = jnp.zeros_like(offset_ref)

    rt = rt_ref[...]
    lane_i = lax.broadcasted_iota(jnp.int32, (tm, LANE), 1)
    lane = lane_i.astype(F32)
    hot = [(lane == rt[:, k:k + 1]).astype(F32) for k in range(TOP_K)]
    both = hot[0] + hot[1]
    before = (lax.broadcasted_iota(jnp.int32, (tm, tm), 1) < lax.broadcasted_iota(jnp.int32, (tm, tm), 0))
    seen = _dot(before.astype(BF16), both.astype(BF16)) + offset_ref[...]
    ranks = [jnp.sum(hot[k] * seen, axis=-1, keepdims=True) for k in range(TOP_K)]
    rank_ref[...] = jnp.where(lane_i == 0, ranks[0], jnp.where(lane_i == 1, ranks[1], 0.0))
    offset_ref[...] += jnp.sum(both, axis=0, keepdims=True)
    count_ref[...] = offset_ref[...]


def _rank(rt):
    T = rt.shape[0]
    tm = 512
    return pl.pallas_call(
        _rank_kernel,
        grid=(T // tm,),
        in_specs=[pl.BlockSpec((tm, LANE), lambda i: (i, 0))],
        out_specs=[pl.BlockSpec((tm, LANE), lambda i: (i, 0)), pl.BlockSpec((1, LANE), lambda i: (0, 0))],
        out_shape=[jax.ShapeDtypeStruct((T, LANE), F32), jax.ShapeDtypeStruct((1, LANE), F32)],
        scratch_shapes=[pltpu.VMEM((1, LANE), F32)],
        compiler_params=_params(("arbitrary",)),
        name="rank",
    )(rt)


def _scatter_kernel(pos_ref, x_ref, xs_in, xs_hbm, sem):
    del xs_in
    tm = x_ref.shape[0] // SLAB

    def copies():
        for r in range(tm):
            for k in range(TOP_K):
                row = pl.multiple_of(pos_ref[0, TOP_K * r + k] * SLAB, SLAB)
                yield pltpu.make_async_copy(x_ref.at[pl.ds(r * SLAB, SLAB), :],
                                            xs_hbm.at[pl.ds(row, SLAB), :], sem.at[0])

    for c in copies():
        c.start()
    for c in copies():
        c.wait()


def _scatter_rows(x1, pos, xs):
    T = pos.shape[0]
    tm = 256
    nt = T // tm
    return pl.pallas_call(
        _scatter_kernel,
        grid=(nt,),
        in_specs=[pl.BlockSpec((None, 1, TOP_K * tm), lambda i: (i, 0, 0), memory_space=pltpu.SMEM),
                  pl.BlockSpec((tm * SLAB, LANE), lambda i: (i, 0)),
                  pl.BlockSpec(memory_space=pl.ANY)],
        out_specs=pl.BlockSpec(memory_space=pl.ANY),
        out_shape=jax.ShapeDtypeStruct(xs.shape, xs.dtype),
        scratch_shapes=[pltpu.SemaphoreType.DMA((1,))],
        input_output_aliases={2: 0},
        compiler_params=_params(("arbitrary",)),
        name="scatter_rows",
    )(pos.reshape(nt, 1, TOP_K * tm), x1, xs)


def _expert_kernel(be_ref, xs_ref, wg_ref, wu_ref, wd_ref, ys_ref, wgb, wub, wdb, h_even, h_odd):
    i = pl.program_id(0)
    last = pl.num_programs(0) - 2
    bm = xs_ref.shape[0] // SLAB
    up, down = jnp.minimum(i, last), jnp.maximum(i - 1, 0)

    @pl.when(jnp.logical_or(i == 0, be_ref[up] != be_ref[jnp.maximum(up - 1, 0)]))
    def _():
        wgb[...] = wg_ref[...].astype(BF16)
        wub[...] = wu_ref[...].astype(BF16)

    @pl.when(jnp.logical_or(i <= 1, be_ref[down] != be_ref[jnp.maximum(down - 1, 0)]))
    def _():
        wdb[...] = wd_ref[...].astype(BF16)

    @pl.when(i == 0)
    def _():
        h_odd[...] = jnp.zeros_like(h_odd)

    def step(h_out, h_in):
        xb = _from_slab(xs_ref, bm).astype(BF16)
        h_out[...] = (_silu(_dot(xb, wgb[...])) * _dot(xb, wub[...])).astype(BF16)
        _to_slab(ys_ref, _dot(h_in[...], wdb[...]))

    @pl.when(i % 2 == 0)
    def _():
        step(h_even, h_odd)

    @pl.when(i % 2 == 1)
    def _():
        step(h_odd, h_even)


def _experts(xs, blk_expert, wg, wu, wd, layer):
    D = SLAB * LANE
    F = wg.shape[3]
    nblk = blk_expert.shape[0]
    bm = MOE_BM
    up = lambda i: jnp.minimum(i, nblk - 1)
    down = lambda i: jnp.maximum(i - 1, 0)
    grid_spec = pltpu.PrefetchScalarGridSpec(
        num_scalar_prefetch=1,
        grid=(nblk + 1,),
        in_specs=[
            pl.BlockSpec((bm * SLAB, LANE), lambda i, be: (up(i), 0)),
            pl.BlockSpec((None, None, D, F), lambda i, be: (layer, be[up(i)], 0, 0)),
            pl.BlockSpec((None, None, D, F), lambda i, be: (layer, be[up(i)], 0, 0)),
            pl.BlockSpec((None, None, F, D), lambda i, be: (layer, be[down(i)], 0, 0)),
        ],
        out_specs=pl.BlockSpec((bm * SLAB, LANE), lambda i, be: (down(i), 0)),
        scratch_shapes=[pltpu.VMEM((D, F), BF16), pltpu.VMEM((D, F), BF16), pltpu.VMEM((F, D), BF16),
                        pltpu.VMEM((bm, F), BF16), pltpu.VMEM((bm, F), BF16)],
    )
    return pl.pallas_call(
        _expert_kernel,
        grid_spec=grid_spec,
        out_shape=jax.ShapeDtypeStruct((nblk * bm * SLAB, LANE), F32),
        compiler_params=_params(("arbitrary",)),
        name="experts",
    )(blk_expert, xs, wg, wu, wd)


_COMBINE_AHEAD = 2


def _combine_kernel(pos0_ref, pos1_ref, pos2_ref, x1_ref, rt_ref, ys_hbm, g_ref, b_ref, x2_ref, x2b_ref, buf, sem):
    i = pl.program_id(0)
    n = pl.num_programs(0)
    nslot = _COMBINE_AHEAD + 1
    slot = i % nslot
    tm = rt_ref.shape[0]

    def gather(pos, sl):
        for r in range(tm):
            for k in range(TOP_K):
                _slab_copy(ys_hbm, pos[0, TOP_K * r + k], buf.at[sl, k], r, sem.at[sl]).start()

    def wait(sl):
        for r in range(tm):
            for k in range(TOP_K):
                _slab_copy(ys_hbm, 0, buf.at[sl, k], r, sem.at[sl]).wait()

    @pl.when(i == 0)
    def _():
        gather(pos0_ref, 0)
        gather(pos1_ref, 1)

    wait(slot)
    rt = rt_ref[...]
    moe = rt[:, 2:3] * _from_slab(buf.at[slot, 0], tm) + rt[:, 3:4] * _from_slab(buf.at[slot, 1], tm)
    gather(pos2_ref, (i + _COMBINE_AHEAD) % nslot)

    x2 = _layer_norm(ALPHA * _from_slab(x1_ref, tm) + moe, g_ref[...], b_ref[...])
    x2_ref[...] = x2
    x2b_ref[...] = x2.astype(BF16)

    @pl.when(i == n - 1)
    def _():
        wait((i + 1) % nslot)
        wait((i + 2) % nslot)


def _combine(x1, rt, pos, ys, g, b):
    T = rt.shape[0]
    D = SLAB * LANE
    tm = 128
    nt = T // tm
    posr = pos.reshape(nt, 1, TOP_K * tm)
    row = lambda n: pl.BlockSpec((tm, n), lambda i: (i, 0))
    fix = lambda shape: pl.BlockSpec(shape, lambda i: (0, 0))
    ahead = lambda d: pl.BlockSpec((None, 1, TOP_K * tm), lambda i: (jnp.minimum(i + d, nt - 1), 0, 0),
                                   memory_space=pltpu.SMEM)
    return pl.pallas_call(
        _combine_kernel,
        grid=(nt,),
        in_specs=[
            ahead(0), ahead(1), ahead(2),
            pl.BlockSpec((tm * SLAB, LANE), lambda i: (i, 0)), row(LANE),
            pl.BlockSpec(memory_space=pl.ANY), fix((1, D)), fix((1, D)),
        ],
        out_specs=[row(D), row(D)],
        out_shape=[jax.ShapeDtypeStruct((T, D), F32), jax.ShapeDtypeStruct((T, D), BF16)],
        scratch_shapes=[pltpu.VMEM((_COMBINE_AHEAD + 1, TOP_K, tm * SLAB, LANE), F32),
                        pltpu.SemaphoreType.DMA((_COMBINE_AHEAD + 1,))],
        compiler_params=_params(("arbitrary",)),
        name="combine",
    )(posr, posr, posr, x1, rt, ys, g, b)


def _rope_tables(S):
    half = HEAD // 2
    inv = ROPE_BASE ** (-jnp.arange(half, dtype=F32) / half)
    ang = jnp.arange(S).astype(F32)[:, None] * inv[None, :]
    cos, sin = jnp.cos(ang), jnp.sin(ang)
    return jnp.concatenate([cos, cos], axis=-1), jnp.concatenate([-sin, sin], axis=-1)


def _ret_tables():
    H, C = RET_HEADS, RET_CHUNK
    log_g = jnp.log1p(-jnp.exp2(-5.0 - jnp.arange(H, dtype=F32)))
    idx = jnp.arange(C, dtype=F32)
    diff = idx[:, None] - idx[None, :]
    intra = jnp.where(diff >= 0, jnp.exp(jnp.maximum(diff, 0.0)[None] * log_g[:, None, None]), 0.0)
    kdec = jnp.exp((C - 1 - idx)[None] * log_g[:, None])
    qdec = jnp.exp((idx + 1)[None] * log_g[:, None])
    cdec = jnp.exp(C * log_g)
    wide = lambda t: jnp.broadcast_to(t[:, :, None], t.shape + (HEAD,))
    return intra, wide(kdec), wide(qdec), wide(cdec[:, None])


def _moe_blocks(n_tok):
    return n_tok * TOP_K // MOE_BM + N_EXPERTS


def _dispatch(rt, rank, counts):
    bm = MOE_BM
    nblk = _moe_blocks(rt.shape[0])
    counts = counts[0, :N_EXPERTS].astype(jnp.int32)
    pcounts = (counts + bm - 1) // bm * bm
    pends = jnp.cumsum(pcounts)
    pstarts = pends - pcounts
    ids = rt[:, :TOP_K].astype(jnp.int32)
    hot = ids[:, :, None] == jnp.arange(N_EXPERTS, dtype=jnp.int32)[None, None, :]
    pos = jnp.sum(jnp.where(hot, pstarts[None, None, :], 0), axis=-1) + rank[:, :TOP_K].astype(jnp.int32)
    blk_start = jnp.arange(nblk, dtype=jnp.int32) * bm
    blk_expert = jnp.sum((pends[None, :] <= blk_start[:, None]).astype(jnp.int32), axis=1)
    return pos, jnp.minimum(blk_expert, N_EXPERTS - 1)


def kernel(x, w_in, w_gla_gate, b_gla_gate, ret_norm_g, gla_norm_g, w_out, ln1_g, ln1_b,
           w_router_group, b_router_group, w_router_expert, b_router_expert,
           w_expert_gate, w_expert_up, w_expert_down, ln2_g, ln2_b):
    B, S, D = x.shape
    T = B * S
    assert w_in.shape[2] == PROJ_REAL and S % (DIL_BLOCK * DIL_DILATIONS[-1]) == 0
    cosf, sinf = _rope_tables(S)
    tables = _ret_tables()
    gate_rank = w_gla_gate.shape[1]
    pairs = GLA_HEADS // 2
    xs = jnp.zeros((_moe_blocks(T) * MOE_BM * SLAB, LANE), F32)

    xf = x.reshape(T, D)
    xb = xf.astype(BF16)
    for l in range(DEPTH):
        w_in_p = jnp.pad(w_in[l].astype(BF16), ((0, 0), (0, PROJ_PAD - PROJ_REAL)))
        proj = _in_proj(xb, w_in_p).reshape(B, S, PROJ_PAD)

        ret = _retention(proj, cosf, sinf, tables, ret_norm_g[l].reshape(RET_HEADS, 1, HEAD))
        dil = _dilated(proj)
        wg = w_gla_gate[l].reshape(gate_rank, pairs, LANE).transpose(1, 0, 2)
        wg = jnp.pad(wg.astype(BF16), ((0, 0), (0, LANE - gate_rank), (0, 0)))
        gla = _gla(proj, wg, b_gla_gate[l].reshape(pairs, 1, LANE),
                   gla_norm_g[l].reshape(pairs, 1, 2 * HEAD))

        wr = jnp.concatenate([w_router_group[l], w_router_expert[l]], axis=1)
        wr = jnp.pad(wr.astype(BF16), ((0, 0), (0, LANE - wr.shape[1])))
        br = jnp.concatenate([b_router_group[l], b_router_expert[l]])
        br = jnp.pad(br, (0, LANE - br.shape[0])).reshape(1, LANE)
        x1, rt = _post_attn(ret.reshape(T, -1), dil.reshape(T, -1), gla.reshape(T, -1), xf,
                            w_out[l].astype(BF16), ln1_g[l].reshape(1, D), ln1_b[l].reshape(1, D), wr, br)

        rank, counts = _rank(rt)
        pos, blk_expert = _dispatch(rt, rank, counts)
        xs = _scatter_rows(x1, pos, xs)
        ys = _experts(xs, blk_expert, w_expert_gate, w_expert_up, w_expert_down, l)
        xf, xb = _combine(x1, rt, pos, ys, ln2_g[l].reshape(1, D), ln2_b[l].reshape(1, D))
    return xf.reshape(B, S, D)
```

```python
import functools

import jax
import jax.numpy as jnp
from jax import lax
from jax.experimental import pallas as pl
from jax.experimental.pallas import tpu as pltpu

F32 = jnp.float32
BF16 = jnp.bfloat16

DEPTH = 4
RET_HEADS = 4
RET_CHUNK = 128
ROPE_BASE = 10000.0
DIL_HEADS = 6
DIL_DILATIONS = (1, 4, 16)
DIL_BLOCK = 128
GLA_HEADS = 6
GLA_DK = 64
GLA_TAU = 16.0
GLA_CHUNK = 64
N_GROUPS = 4
EXPERTS_PER_GROUP = 8
N_EXPERTS = N_GROUPS * EXPERTS_PER_GROUP
TOP_K = 2
LN_EPS = 1e-5
ALPHA = (2 * DEPTH) ** 0.25

LANE = 128
HEAD = 128
OFF_RQ, OFF_RK, OFF_RV, OFF_RG = 0, 4, 8, 12
OFF_DQ, OFF_DK, OFF_DV = 16, 22, 28
OFF_GQ, OFF_GK = 34, 37
OFF_GV, OFF_GR = 20, 23
OFF_GA = 52
PROJ_REAL = 6672
PROJ_PAD = 6912
MOE_BM = 256
VMEM_LIMIT = 48 * 1024 * 1024


def _dot(a, b):
    return jnp.dot(a, b, preferred_element_type=F32)


def _dot_nt(a, b):
    return lax.dot_general(a, b, (((1,), (1,)), ((), ())), preferred_element_type=F32)


def _dot_tn(a, b):
    return lax.dot_general(a, b, (((0,), (0,)), ((), ())), preferred_element_type=F32)


def _params(sem):
    return pltpu.CompilerParams(dimension_semantics=sem, vmem_limit_bytes=VMEM_LIMIT)


def _silu(g):
    return g / (1.0 + jnp.exp(-g))


def _head_norm(o):
    mu = jnp.mean(o, axis=-1, keepdims=True)
    d = o - mu
    var = jnp.mean(d * d, axis=-1, keepdims=True)
    return d * lax.rsqrt(var + LN_EPS)


def _mm_kernel(x_ref, w_ref, o_ref):
    o_ref[...] = _dot(x_ref[...], w_ref[...])


def _in_proj(xb, wb):
    T, K = xb.shape
    N = wb.shape[1]
    tm, tn = 1024, 768
    return pl.pallas_call(
        _mm_kernel,
        grid=(T // tm, N // tn),
        in_specs=[pl.BlockSpec((tm, K), lambda i, j: (i, 0)),
                  pl.BlockSpec((K, tn), lambda i, j: (0, j))],
        out_specs=pl.BlockSpec((tm, tn), lambda i, j: (i, j)),
        out_shape=jax.ShapeDtypeStruct((T, N), F32),
        compiler_params=_params(("parallel", "arbitrary")),
        name="in_proj",
    )(xb, wb)


def _ret_kernel(q_ref, k_ref, v_ref, g_ref, cos_ref, sin_ref, intra_ref, kdec_ref, qdec_ref,
                cdec_ref, gain_ref, o_ref, state_ref):
    S = q_ref.shape[0]
    C = RET_CHUNK
    scale = HEAD ** -0.5
    state_ref[...] = jnp.zeros_like(state_ref)

    def chunk(n, carry):
        rows = pl.ds(pl.multiple_of(n * C, C), C)
        cos = cos_ref[rows, :]
        sin = sin_ref[rows, :]
        q = q_ref[rows, :]
        k = k_ref[rows, :]
        qr = q * cos + pltpu.roll(q, HEAD // 2, 1) * sin
        kr = (k * cos + pltpu.roll(k, HEAD // 2, 1) * sin) * scale
        v = v_ref[rows, :].astype(BF16)
        scores = _dot_nt(qr.astype(BF16), kr.astype(BF16)) * intra_ref[...]
        state = state_ref[...]
        o = _dot(scores.astype(BF16), v) + _dot((qr * qdec_ref[...]).astype(BF16), state.astype(BF16))
        state_ref[...] = state * cdec_ref[...] + _dot_tn((kr * kdec_ref[...]).astype(BF16), v)
        g = g_ref[rows, :]
        o_ref[rows, :] = (_silu(g) * (_head_norm(o) * gain_ref[...])).astype(o_ref.dtype)
        return carry

    lax.fori_loop(0, S // C, chunk, 0, unroll=8)


def _retention(proj, cosf, sinf, tables, gain):
    B, S, _ = proj.shape
    H, C = RET_HEADS, RET_CHUNK
    intra, kdec, qdec, cdec = tables

    def col(off):
        return pl.BlockSpec((None, S, HEAD), lambda b, h: (b, 0, off + h))

    full = pl.BlockSpec((S, HEAD), lambda b, h: (0, 0))
    per_head = lambda r: pl.BlockSpec((None, r, HEAD), lambda b, h: (h, 0, 0))
    return pl.pallas_call(
        _ret_kernel,
        grid=(B, H),
        in_specs=[col(OFF_RQ), col(OFF_RK), col(OFF_RV), col(OFF_RG), full, full,
                  pl.BlockSpec((None, C, C), lambda b, h: (h, 0, 0)),
                  per_head(C), per_head(C), per_head(1), per_head(1)],
        out_specs=pl.BlockSpec((None, S, HEAD), lambda b, h: (b, 0, h)),
        out_shape=jax.ShapeDtypeStruct((B, S, H * HEAD), BF16),
        scratch_shapes=[pltpu.VMEM((HEAD, HEAD), F32)],
        compiler_params=_params(("parallel", "parallel")),
        name="retention",
    )(proj, proj, proj, proj, cosf, sinf, intra, kdec, qdec, cdec, gain)


def _dil_kernel(q_ref, k_ref, v_ref, o_ref, acc, m, s):
    S = q_ref.shape[0]
    Q = DIL_BLOCK
    scale = HEAD ** -0.5
    neg = jnp.finfo(F32).min
    ii = lax.broadcasted_iota(jnp.int32, (Q, 2 * Q), 0)
    jj = lax.broadcasted_iota(jnp.int32, (Q, 2 * Q), 1)
    cur_ok = jnp.logical_and(jj >= Q, jj - Q <= ii)
    prev_ok = jnp.logical_and(jj < Q, jj >= ii)
    order = tuple(reversed(DIL_DILATIONS))

    for pi, d in enumerate(order):
        nb = S // (d * Q)
        shift = nb.bit_length() - 1
        first, last = pi == 0, pi == len(order) - 1

        def load(ref, start, d=d):
            if d == 1:
                return ref[pl.ds(pl.multiple_of(start, Q), Q), :]
            return ref.at[pl.ds(start, (Q - 1) * d + 1), :][pl.ds(0, Q, stride=d), :]

        def store(ref, start, val, d=d):
            if d == 1:
                ref[pl.ds(pl.multiple_of(start, Q), Q), :] = val
            else:
                ref.at[pl.ds(start, (Q - 1) * d + 1), :][pl.ds(0, Q, stride=d), :] = val

        def pair(jt, carry, d=d, nb=nb, shift=shift, first=first, last=last, load=load, store=store):
            it = 2 * jt
            n0 = it & (nb - 1)
            r = it >> shift
            start = r + d * Q * n0
            has_prev = n0 > 0
            starts = (jnp.maximum(start - d * Q, r), start, start + d * Q)
            ks = [load(k_ref, st).astype(BF16) for st in starts]
            vs = [load(v_ref, st).astype(BF16) for st in starts]
            for b in range(2):
                qs = starts[b + 1]
                q = load(q_ref, qs).astype(BF16)
                sc = _dot_nt(q, jnp.concatenate([ks[b], ks[b + 1]], axis=0)) * scale
                ok = jnp.logical_or(cur_ok, jnp.logical_and(prev_ok, has_prev)) if b == 0 else jnp.logical_or(cur_ok, prev_ok)
                sc = jnp.where(ok, sc, neg)
                mx = jnp.max(sc, axis=-1, keepdims=True)
                p = jnp.exp(sc - mx)
                a = _dot(p.astype(BF16), jnp.concatenate([vs[b], vs[b + 1]], axis=0))
                m_new = jnp.broadcast_to(mx, (Q, HEAD))
                s_new = jnp.broadcast_to(jnp.sum(p, axis=-1, keepdims=True), (Q, HEAD))
                if first:
                    store(acc, qs, a)
                    store(m, qs, m_new)
                    store(s, qs, s_new)
                    continue
                m_old = load(m, qs)
                m_all = jnp.maximum(m_old, m_new)
                e_old = jnp.exp(m_old - m_all)
                e_new = jnp.exp(m_new - m_all)
                a = load(acc, qs) * e_old + a * e_new
                s_all = load(s, qs) * e_old + s_new * e_new
                if last:
                    store(o_ref, qs, (a / s_all).astype(o_ref.dtype))
                else:
                    store(acc, qs, a)
                    store(m, qs, m_all)
                    store(s, qs, s_all)
            return carry

        lax.fori_loop(0, S // (2 * Q), pair, 0, unroll=8)


def _dilated(proj):
    B, S, _ = proj.shape
    H = DIL_HEADS

    def col(off):
        return pl.BlockSpec((None, S, HEAD), lambda b, h: (b, 0, off + h))

    return pl.pallas_call(
        _dil_kernel,
        grid=(B, H),
        in_specs=[col(OFF_DQ), col(OFF_DK), col(OFF_DV)],
        out_specs=pl.BlockSpec((None, S, HEAD), lambda b, h: (b, 0, h)),
        out_shape=jax.ShapeDtypeStruct((B, S, H * HEAD), BF16),
        scratch_shapes=[pltpu.VMEM((S, HEAD), F32)] * 3,
        compiler_params=_params(("parallel", "parallel")),
        name="dilated",
    )(proj, proj, proj)


_GLA_GATE_ROWS = 512


def _gla_kernel(q_ref, k_ref, v_ref, r_ref, ga_ref, wg_ref, bg_ref, gain_ref, o_ref, la_ref, st_ref):
    S = q_ref.shape[0]
    C = GLA_CHUNK
    for c in range(S // _GLA_GATE_ROWS):
        rows = pl.ds(c * _GLA_GATE_ROWS, _GLA_GATE_ROWS)
        z = _dot(ga_ref[rows, :].astype(BF16), wg_ref[...]) + bg_ref[...]
        la_ref[rows, :] = (jnp.minimum(z, 0.0) - jnp.log1p(jnp.exp(-jnp.abs(z)))) * (1.0 / GLA_TAU)
    st_ref[...] = jnp.zeros_like(st_ref)
    row_i = lax.broadcasted_iota(jnp.int32, (C, LANE), 0)
    lane = lax.broadcasted_iota(jnp.int32, (C, LANE), 1)
    tril = lax.broadcasted_iota(jnp.int32, (C, C), 1) <= lax.broadcasted_iota(jnp.int32, (C, C), 0)

    def chunk(n, carry):
        rows = pl.ds(pl.multiple_of(n * C, C), C)
        b = la_ref[rows, :]
        sh = 1
        while sh < C:
            b = b + jnp.where(row_i >= sh, pltpu.roll(b, sh, 0), 0.0)
            sh *= 2
        q = q_ref[rows, :] * (GLA_DK ** -0.5)
        k = k_ref[rows, :]
        b_last = b[C - 1:C, :]
        qe = q * jnp.exp(b)
        ke = (k * jnp.exp(-b)).astype(BF16)
        kd = k * jnp.exp(b_last - b)
        decay = jnp.exp(b_last)
        for hh in range(2):
            mine = jnp.logical_and(lane >= GLA_DK * hh, lane < GLA_DK * (hh + 1))
            cols = pl.ds(HEAD * hh, HEAD)
            qh = jnp.where(mine, qe, 0.0).astype(BF16)
            kh = jnp.where(mine, kd, 0.0).astype(BF16)
            vh = v_ref[rows, cols].astype(BF16)
            a = jnp.where(tril, _dot_nt(qh, ke), 0.0)
            st = st_ref[hh]
            o = _dot(a.astype(BF16), vh) + _dot_nt(qh, st.astype(BF16))
            st_ref[hh] = st * decay + _dot_tn(vh, kh)
            g = r_ref[rows, cols]
            o_ref[rows, cols] = (_silu(g) * (_head_norm(o) * gain_ref[:, cols])).astype(o_ref.dtype)
        return carry

    lax.fori_loop(0, S // C, chunk, 0, unroll=8)


def _gla(proj, wg, bg, gain):
    B, S, _ = proj.shape
    P = GLA_HEADS // 2
    one = lambda off: pl.BlockSpec((None, S, LANE), lambda b, p: (b, 0, off + p))
    two = lambda off: pl.BlockSpec((None, S, 2 * LANE), lambda b, p: (b, 0, off + p))
    return pl.pallas_call(
        _gla_kernel,
        grid=(B, P),
        in_specs=[one(OFF_GQ), one(OFF_GK), two(OFF_GV), two(OFF_GR),
                  pl.BlockSpec((None, S, LANE), lambda b, p: (b, 0, OFF_GA)),
                  pl.BlockSpec((None, LANE, LANE), lambda b, p: (p, 0, 0)),
                  pl.BlockSpec((None, 1, LANE), lambda b, p: (p, 0, 0)),
                  pl.BlockSpec((None, 1, 2 * LANE), lambda b, p: (p, 0, 0))],
        out_specs=pl.BlockSpec((None, S, 2 * LANE), lambda b, p: (b, 0, p)),
        out_shape=jax.ShapeDtypeStruct((B, S, GLA_HEADS * HEAD), BF16),
        scratch_shapes=[pltpu.VMEM((S, LANE), F32), pltpu.VMEM((2, HEAD, LANE), F32)],
        compiler_params=_params(("parallel", "parallel")),
        name="gla",
    )(proj, proj, proj, proj, proj, wg, bg, gain)


def _layer_norm(y, g, b):
    mu = jnp.mean(y, axis=-1, keepdims=True)
    d = y - mu
    var = jnp.mean(d * d, axis=-1, keepdims=True)
    return d * lax.rsqrt(var + LN_EPS) * g + b


SLAB = 16


def _to_slab(ref, val):
    n = val.shape[0]
    for j in range(SLAB):
        ref[pl.ds(j, n, stride=SLAB), :] = val[:, j * LANE:(j + 1) * LANE]


def _from_slab(ref, n):
    return jnp.concatenate([ref[pl.ds(j, n, stride=SLAB), :] for j in range(SLAB)], axis=1)


_POST_SUB = 256


def _post_kernel(ret_ref, dil_ref, gla_ref, x_ref, w_ref, g_ref, b_ref, wr_ref, br_ref, x1_ref, rt_ref):
    for s in range(x_ref.shape[0] // _POST_SUB):
        rows = pl.ds(s * _POST_SUB, _POST_SUB)
        _post_rows(ret_ref.at[rows, :], dil_ref.at[rows, :], gla_ref.at[rows, :], x_ref.at[rows, :], w_ref, g_ref,
                   b_ref, wr_ref, br_ref, x1_ref.at[pl.ds(s * _POST_SUB * SLAB, _POST_SUB * SLAB), :], rt_ref.at[rows, :])


def _post_rows(ret_ref, dil_ref, gla_ref, x_ref, w_ref, g_ref, b_ref, wr_ref, br_ref, x1_ref, rt_ref):
    n_ret = ret_ref.shape[1]
    n_dil = dil_ref.shape[1]
    mixed = (_dot(ret_ref[...], w_ref[0:n_ret, :])
             + _dot(dil_ref[...], w_ref[n_ret:n_ret + n_dil, :])
             + _dot(gla_ref[...], w_ref[n_ret + n_dil:, :]))
    x1 = _layer_norm(ALPHA * x_ref[...] + mixed, g_ref[...], b_ref[...])
    _to_slab(x1_ref, x1)

    logits = _dot(x1.astype(BF16), wr_ref[...]) + br_ref[...]
    lane_i = lax.broadcasted_iota(jnp.int32, logits.shape, 1)
    lane = lane_i.astype(F32)
    far = float(LANE)
    ninf = -jnp.inf
    gmask = lane_i < N_GROUPS
    gmax = jnp.max(jnp.where(gmask, logits, ninf), axis=-1, keepdims=True)
    gsel = jnp.min(jnp.where(jnp.logical_and(gmask, logits == gmax), lane, far), axis=-1, keepdims=True)
    g_w = 1.0 / jnp.sum(jnp.where(gmask, jnp.exp(logits - gmax), 0.0), axis=-1, keepdims=True)
    lo = N_GROUPS + EXPERTS_PER_GROUP * gsel
    emask = jnp.logical_and(lane >= lo, lane < lo + EXPERTS_PER_GROUP)
    v1 = jnp.max(jnp.where(emask, logits, ninf), axis=-1, keepdims=True)
    i1 = jnp.min(jnp.where(jnp.logical_and(emask, logits == v1), lane, far), axis=-1, keepdims=True)
    emask2 = jnp.logical_and(emask, lane != i1)
    v2 = jnp.max(jnp.where(emask2, logits, ninf), axis=-1, keepdims=True)
    i2 = jnp.min(jnp.where(jnp.logical_and(emask2, logits == v2), lane, far), axis=-1, keepdims=True)
    t = jnp.exp(v2 - v1)
    gate1 = g_w / (1.0 + t)
    gate2 = g_w * t / (1.0 + t)
    rt_ref[...] = jnp.where(lane_i == 0, i1 - N_GROUPS,
                            jnp.where(lane_i == 1, i2 - N_GROUPS,
                                      jnp.where(lane_i == 2, gate1,
                                                jnp.where(lane_i == 3, gate2, 0.0))))


def _post_attn(ret, dil, gla, x, w, g, b, wr, br):
    T, D = x.shape
    tm = 2 * _POST_SUB
    row = lambda n: pl.BlockSpec((tm, n), lambda i: (i, 0))
    fix = lambda shape: pl.BlockSpec(shape, lambda i: (0, 0))
    return pl.pallas_call(
        _post_kernel,
        grid=(T // tm,),
        in_specs=[row(ret.shape[1]), row(dil.shape[1]), row(gla.shape[1]), row(D),
                  fix(w.shape), fix((1, D)), fix((1, D)), fix(wr.shape), fix((1, LANE))],
        out_specs=[pl.BlockSpec((tm * SLAB, LANE), lambda i: (i, 0)), row(LANE)],
        out_shape=[jax.ShapeDtypeStruct((T * SLAB, LANE), F32), jax.ShapeDtypeStruct((T, LANE), F32)],
        compiler_params=_params(("parallel",)),
        name="post_attn",
    )(ret, dil, gla, x, w, g, b, wr, br)


def _slab_copy(src_hbm, row, dst, r, sem):
    return pltpu.make_async_copy(src_hbm.at[pl.ds(pl.multiple_of(row * SLAB, SLAB), SLAB), :],
                                 dst.at[pl.ds(r * SLAB, SLAB), :], sem)


def _rank_kernel(rt_ref, rank_ref, count_ref, offset_ref):
    i = pl.program_id(0)
    tm = rt_ref.shape[0]

    @pl.when(i == 0)
    def _():
        offset_ref[...] = jnp.zeros_like(offset_ref)

    rt = rt_ref[...]
    lane_i = lax.broadcasted_iota(jnp.int32, (tm, LANE), 1)
    lane = lane_i.astype(F32)
    hot = [(lane == rt[:, k:k + 1]).astype(F32) for k in range(TOP_K)]
    both = hot[0] + hot[1]
    before = (lax.broadcasted_iota(jnp.int32, (tm, tm), 1) < lax.broadcasted_iota(jnp.int32, (tm, tm), 0))
    seen = _dot(before.astype(BF16), both.astype(BF16)) + offset_ref[...]
    ranks = [jnp.sum(hot[k] * seen, axis=-1, keepdims=True) for k in range(TOP_K)]
    rank_ref[...] = jnp.where(lane_i == 0, ranks[0], jnp.where(lane_i == 1, ranks[1], 0.0))
    offset_ref[...] += jnp.sum(both, axis=0, keepdims=True)
    count_ref[...] = offset_ref[...]


def _rank(rt):
    T = rt.shape[0]
    tm = 512
    return pl.pallas_call(
        _rank_kernel,
        grid=(T // tm,),
        in_specs=[pl.BlockSpec((tm, LANE), lambda i: (i, 0))],
        out_specs=[pl.BlockSpec((tm, LANE), lambda i: (i, 0)), pl.BlockSpec((1, LANE), lambda i: (0, 0))],
        out_shape=[jax.ShapeDtypeStruct((T, LANE), F32), jax.ShapeDtypeStruct((1, LANE), F32)],
        scratch_shapes=[pltpu.VMEM((1, LANE), F32)],
        compiler_params=_params(("arbitrary",)),
        name="rank",
    )(rt)


def _scatter_kernel(pos_ref, x_ref, xs_in, xs_hbm, sem):
    del xs_in
    tm = x_ref.shape[0] // SLAB

    def copies():
        for r in range(tm):
            for k in range(TOP_K):
                row = pl.multiple_of(pos_ref[0, TOP_K * r + k] * SLAB, SLAB)
                yield pltpu.make_async_copy(x_ref.at[pl.ds(r * SLAB, SLAB), :],
                                            xs_hbm.at[pl.ds(row, SLAB), :], sem.at[0])

    for c in copies():
        c.start()
    for c in copies():
        c.wait()


def _scatter_rows(x1, pos, xs):
    T = pos.shape[0]
    tm = 512
    nt = T // tm
    return pl.pallas_call(
        _scatter_kernel,
        grid=(nt,),
        in_specs=[pl.BlockSpec((None, 1, TOP_K * tm), lambda i: (i, 0, 0), memory_space=pltpu.SMEM),
                  pl.BlockSpec((tm * SLAB, LANE), lambda i: (i, 0)),
                  pl.BlockSpec(memory_space=pl.ANY)],
        out_specs=pl.BlockSpec(memory_space=pl.ANY),
        out_shape=jax.ShapeDtypeStruct(xs.shape, xs.dtype),
        scratch_shapes=[pltpu.SemaphoreType.DMA((1,))],
        input_output_aliases={2: 0},
        compiler_params=_params(("arbitrary",)),
        name="scatter_rows",
    )(pos.reshape(nt, 1, TOP_K * tm), x1, xs)


def _expert_kernel(be_ref, xs_ref, wg_ref, wu_ref, wd_ref, ys_ref, wgb, wub, wdb, h_even, h_odd):
    i = pl.program_id(0)
    last = pl.num_programs(0) - 2
    bm = xs_ref.shape[0] // SLAB
    up, down = jnp.minimum(i, last), jnp.maximum(i - 1, 0)

    @pl.when(jnp.logical_or(i == 0, be_ref[up] != be_ref[jnp.maximum(up - 1, 0)]))
    def _():
        wgb[...] = wg_ref[...].astype(BF16)
        wub[...] = wu_ref[...].astype(BF16)

    @pl.when(jnp.logical_or(i <= 1, be_ref[down] != be_ref[jnp.maximum(down - 1, 0)]))
    def _():
        wdb[...] = wd_ref[...].astype(BF16)

    @pl.when(i == 0)
    def _():
        h_odd[...] = jnp.zeros_like(h_odd)

    def step(h_out, h_in):
        xb = _from_slab(xs_ref, bm).astype(BF16)
        h_out[...] = (_silu(_dot(xb, wgb[...])) * _dot(xb, wub[...])).astype(BF16)
        _to_slab(ys_ref, _dot(h_in[...], wdb[...]))

    @pl.when(i % 2 == 0)
    def _():
        step(h_even, h_odd)

    @pl.when(i % 2 == 1)
    def _():
        step(h_odd, h_even)


def _experts(xs, blk_expert, wg, wu, wd, layer):
    D = SLAB * LANE
    F = wg.shape[3]
    nblk = blk_expert.shape[0]
    bm = MOE_BM
    up = lambda i: jnp.minimum(i, nblk - 1)
    down = lambda i: jnp.maximum(i - 1, 0)
    grid_spec = pltpu.PrefetchScalarGridSpec(
        num_scalar_prefetch=1,
        grid=(nblk + 1,),
        in_specs=[
            pl.BlockSpec((bm * SLAB, LANE), lambda i, be: (up(i), 0)),
            pl.BlockSpec((None, None, D, F), lambda i, be: (layer, be[up(i)], 0, 0)),
            pl.BlockSpec((None, None, D, F), lambda i, be: (layer, be[up(i)], 0, 0)),
            pl.BlockSpec((None, None, F, D), lambda i, be: (layer, be[down(i)], 0, 0)),
        ],
        out_specs=pl.BlockSpec((bm * SLAB, LANE), lambda i, be: (down(i), 0)),
        scratch_shapes=[pltpu.VMEM((D, F), BF16), pltpu.VMEM((D, F), BF16), pltpu.VMEM((F, D), BF16),
                        pltpu.VMEM((bm, F), BF16), pltpu.VMEM((bm, F), BF16)],
    )
    return pl.pallas_call(
        _expert_kernel,
        grid_spec=grid_spec,
        out_shape=jax.ShapeDtypeStruct((nblk * bm * SLAB, LANE), F32),
        compiler_params=_params(("arbitrary",)),
        name="experts",
    )(blk_expert, xs, wg, wu, wd)


_COMBINE_AHEAD = 3


def _combine_kernel(*refs):
    pos_refs = refs[:_COMBINE_AHEAD + 1]
    x1_ref, rt_ref, ys_hbm, g_ref, b_ref, x2_ref, x2b_ref, buf, sem = refs[_COMBINE_AHEAD + 1:]
    i = pl.program_id(0)
    n = pl.num_programs(0)
    nslot = _COMBINE_AHEAD + 1
    slot = i % nslot
    tm = rt_ref.shape[0]

    def gather(pos, sl):
        for r in range(tm):
            for k in range(TOP_K):
                _slab_copy(ys_hbm, pos[0, TOP_K * r + k], buf.at[sl, k], r, sem.at[sl]).start()

    def wait(sl):
        for r in range(tm):
            for k in range(TOP_K):
                _slab_copy(ys_hbm, 0, buf.at[sl, k], r, sem.at[sl]).wait()

    @pl.when(i == 0)
    def _():
        for d in range(_COMBINE_AHEAD):
            gather(pos_refs[d], d)

    wait(slot)
    rt = rt_ref[...]
    moe = rt[:, 2:3] * _from_slab(buf.at[slot, 0], tm) + rt[:, 3:4] * _from_slab(buf.at[slot, 1], tm)
    gather(pos_refs[_COMBINE_AHEAD], (i + _COMBINE_AHEAD) % nslot)

    x2 = _layer_norm(ALPHA * _from_slab(x1_ref, tm) + moe, g_ref[...], b_ref[...])
    x2_ref[...] = x2
    x2b_ref[...] = x2.astype(BF16)

    @pl.when(i == n - 1)
    def _():
        for d in range(1, nslot):
            wait((i + d) % nslot)


def _combine(x1, rt, pos, ys, g, b):
    T = rt.shape[0]
    D = SLAB * LANE
    tm = 128
    nt = T // tm
    posr = pos.reshape(nt, 1, TOP_K * tm)
    row = lambda n: pl.BlockSpec((tm, n), lambda i: (i, 0))
    fix = lambda shape: pl.BlockSpec(shape, lambda i: (0, 0))
    ahead = lambda d: pl.BlockSpec((None, 1, TOP_K * tm), lambda i: (jnp.minimum(i + d, nt - 1), 0, 0),
                                   memory_space=pltpu.SMEM)
    return pl.pallas_call(
        _combine_kernel,
        grid=(nt,),
        in_specs=[ahead(d) for d in range(_COMBINE_AHEAD + 1)] + [
            pl.BlockSpec((tm * SLAB, LANE), lambda i: (i, 0)), row(LANE),
            pl.BlockSpec(memory_space=pl.ANY), fix((1, D)), fix((1, D)),
        ],
        out_specs=[row(D), row(D)],
        out_shape=[jax.ShapeDtypeStruct((T, D), F32), jax.ShapeDtypeStruct((T, D), BF16)],
        scratch_shapes=[pltpu.VMEM((_COMBINE_AHEAD + 1, TOP_K, tm * SLAB, LANE), F32),
                        pltpu.SemaphoreType.DMA((_COMBINE_AHEAD + 1,))],
        compiler_params=_params(("arbitrary",)),
        name="combine",
    )(*([posr] * (_COMBINE_AHEAD + 1)), x1, rt, ys, g, b)


def _rope_tables(S):
    half = HEAD // 2
    inv = ROPE_BASE ** (-jnp.arange(half, dtype=F32) / half)
    ang = jnp.arange(S).astype(F32)[:, None] * inv[None, :]
    cos, sin = jnp.cos(ang), jnp.sin(ang)
    return jnp.concatenate([cos, cos], axis=-1), jnp.concatenate([-sin, sin], axis=-1)


def _ret_tables():
    H, C = RET_HEADS, RET_CHUNK
    log_g = jnp.log1p(-jnp.exp2(-5.0 - jnp.arange(H, dtype=F32)))
    idx = jnp.arange(C, dtype=F32)
    diff = idx[:, None] - idx[None, :]
    intra = jnp.where(diff >= 0, jnp.exp(jnp.maximum(diff, 0.0)[None] * log_g[:, None, None]), 0.0)
    kdec = jnp.exp((C - 1 - idx)[None] * log_g[:, None])
    qdec = jnp.exp((idx + 1)[None] * log_g[:, None])
    cdec = jnp.exp(C * log_g)
    wide = lambda t: jnp.broadcast_to(t[:, :, None], t.shape + (HEAD,))
    return intra, wide(kdec), wide(qdec), wide(cdec[:, None])


def _moe_blocks(n_tok):
    return n_tok * TOP_K // MOE_BM + N_EXPERTS


def _dispatch(rt, rank, counts):
    bm = MOE_BM
    nblk = _moe_blocks(rt.shape[0])
    counts = counts[0, :N_EXPERTS].astype(jnp.int32)
    pcounts = (counts + bm - 1) // bm * bm
    pends = jnp.cumsum(pcounts)
    pstarts = pends - pcounts
    ids = rt[:, :TOP_K].astype(jnp.int32)
    hot = ids[:, :, None] == jnp.arange(N_EXPERTS, dtype=jnp.int32)[None, None, :]
    pos = jnp.sum(jnp.where(hot, pstarts[None, None, :], 0), axis=-1) + rank[:, :TOP_K].astype(jnp.int32)
    blk_start = jnp.arange(nblk, dtype=jnp.int32) * bm
    blk_expert = jnp.sum((pends[None, :] <= blk_start[:, None]).astype(jnp.int32), axis=1)
    return pos, jnp.minimum(blk_expert, N_EXPERTS - 1)


def kernel(x, w_in, w_gla_gate, b_gla_gate, ret_norm_g, gla_norm_g, w_out, ln1_g, ln1_b,
           w_router_group, b_router_group, w_router_expert, b_router_expert,
           w_expert_gate, w_expert_up, w_expert_down, ln2_g, ln2_b):
    B, S, D = x.shape
    T = B * S
    assert w_in.shape[2] == PROJ_REAL and S % (DIL_BLOCK * DIL_DILATIONS[-1]) == 0
    cosf, sinf = _rope_tables(S)
    tables = _ret_tables()
    gate_rank = w_gla_gate.shape[1]
    pairs = GLA_HEADS // 2
    xs = jnp.zeros((_moe_blocks(T) * MOE_BM * SLAB, LANE), F32)

    xf = x.reshape(T, D)
    xb = xf.astype(BF16)
    for l in range(DEPTH):
        w_in_p = jnp.pad(w_in[l].astype(BF16), ((0, 0), (0, PROJ_PAD - PROJ_REAL)))
        proj = _in_proj(xb, w_in_p).reshape(B, S, PROJ_PAD)

        ret = _retention(proj, cosf, sinf, tables, ret_norm_g[l].reshape(RET_HEADS, 1, HEAD))
        dil = _dilated(proj)
        wg = w_gla_gate[l].reshape(gate_rank, pairs, LANE).transpose(1, 0, 2)
        wg = jnp.pad(wg.astype(BF16), ((0, 0), (0, LANE - gate_rank), (0, 0)))
        gla = _gla(proj, wg, b_gla_gate[l].reshape(pairs, 1, LANE),
                   gla_norm_g[l].reshape(pairs, 1, 2 * HEAD))

        wr = jnp.concatenate([w_router_group[l], w_router_expert[l]], axis=1)
        wr = jnp.pad(wr.astype(BF16), ((0, 0), (0, LANE - wr.shape[1])))
        br = jnp.concatenate([b_router_group[l], b_router_expert[l]])
        br = jnp.pad(br, (0, LANE - br.shape[0])).reshape(1, LANE)
        x1, rt = _post_attn(ret.reshape(T, -1), dil.reshape(T, -1), gla.reshape(T, -1), xf,
                            w_out[l].astype(BF16), ln1_g[l].reshape(1, D), ln1_b[l].reshape(1, D), wr, br)

        rank, counts = _rank(rt)
        pos, blk_expert = _dispatch(rt, rank, counts)
        xs = _scatter_rows(x1, pos, xs)
        ys = _experts(xs, blk_expert, w_expert_gate, w_expert_up, w_expert_down, l)
        xf, xb = _combine(x1, rt, pos, ys, ln2_g[l].reshape(1, D), ln2_b[l].reshape(1, D))
    return xf.reshape(B, S, D)
```

```python
import functools

import jax
import jax.numpy as jnp
from jax import lax
from jax.experimental import pallas as pl
from jax.experimental.pallas import tpu as pltpu

F32 = jnp.float32
BF16 = jnp.bfloat16

DEPTH = 4
RET_HEADS = 4
RET_CHUNK = 128
ROPE_BASE = 10000.0
DIL_HEADS = 6
DIL_DILATIONS = (1, 4, 16)
DIL_BLOCK = 128
GLA_HEADS = 6
GLA_DK = 64
GLA_TAU = 16.0
GLA_CHUNK = 64
N_GROUPS = 4
EXPERTS_PER_GROUP = 8
N_EXPERTS = N_GROUPS * EXPERTS_PER_GROUP
TOP_K = 2
LN_EPS = 1e-5
ALPHA = (2 * DEPTH) ** 0.25

LANE = 128
HEAD = 128
OFF_RQ, OFF_RK, OFF_RV, OFF_RG = 0, 4, 8, 12
OFF_DQ, OFF_DK, OFF_DV = 16, 22, 28
OFF_GQ, OFF_GK = 34, 37
OFF_GV, OFF_GR = 20, 23
OFF_GA = 52
PROJ_REAL = 6672
PROJ_PAD = 6912
MOE_BM = 256
VMEM_LIMIT = 48 * 1024 * 1024


def _dot(a, b):
    return jnp.dot(a, b, preferred_element_type=F32)


def _dot_nt(a, b):
    return lax.dot_general(a, b, (((1,), (1,)), ((), ())), preferred_element_type=F32)


def _dot_tn(a, b):
    return lax.dot_general(a, b, (((0,), (0,)), ((), ())), preferred_element_type=F32)


def _params(sem):
    return pltpu.CompilerParams(dimension_semantics=sem, vmem_limit_bytes=VMEM_LIMIT)


def _silu(g):
    return g / (1.0 + jnp.exp(-g))


def _head_norm(o):
    mu = jnp.mean(o, axis=-1, keepdims=True)
    d = o - mu
    var = jnp.mean(d * d, axis=-1, keepdims=True)
    return d * lax.rsqrt(var + LN_EPS)


def _mm_kernel(x_ref, w_ref, o_ref):
    o_ref[...] = _dot(x_ref[...], w_ref[...])


def _in_proj(xb, wb):
    T, K = xb.shape
    N = wb.shape[1]
    tm, tn = 1024, 768
    return pl.pallas_call(
        _mm_kernel,
        grid=(T // tm, N // tn),
        in_specs=[pl.BlockSpec((tm, K), lambda i, j: (i, 0)),
                  pl.BlockSpec((K, tn), lambda i, j: (0, j))],
        out_specs=pl.BlockSpec((tm, tn), lambda i, j: (i, j)),
        out_shape=jax.ShapeDtypeStruct((T, N), F32),
        compiler_params=_params(("parallel", "arbitrary")),
        name="in_proj",
    )(xb, wb)


def _ret_kernel(q_ref, k_ref, v_ref, g_ref, cos_ref, sin_ref, intra_ref, kdec_ref, qdec_ref,
                cdec_ref, gain_ref, o_ref, state_ref):
    S = q_ref.shape[0]
    C = RET_CHUNK
    scale = HEAD ** -0.5
    state_ref[...] = jnp.zeros_like(state_ref)

    def chunk(n, carry):
        rows = pl.ds(pl.multiple_of(n * C, C), C)
        cos = cos_ref[rows, :]
        sin = sin_ref[rows, :]
        q = q_ref[rows, :]
        k = k_ref[rows, :]
        qr = q * cos + pltpu.roll(q, HEAD // 2, 1) * sin
        kr = (k * cos + pltpu.roll(k, HEAD // 2, 1) * sin) * scale
        v = v_ref[rows, :].astype(BF16)
        scores = _dot_nt(qr.astype(BF16), kr.astype(BF16)) * intra_ref[...]
        state = state_ref[...]
        o = _dot(scores.astype(BF16), v) + _dot((qr * qdec_ref[...]).astype(BF16), state.astype(BF16))
        state_ref[...] = state * cdec_ref[...] + _dot_tn((kr * kdec_ref[...]).astype(BF16), v)
        g = g_ref[rows, :]
        o_ref[rows, :] = (_silu(g) * (_head_norm(o) * gain_ref[...])).astype(o_ref.dtype)
        return carry

    lax.fori_loop(0, S // C, chunk, 0, unroll=8)


def _retention(proj, cosf, sinf, tables, gain):
    B, S, _ = proj.shape
    H, C = RET_HEADS, RET_CHUNK
    intra, kdec, qdec, cdec = tables

    def col(off):
        return pl.BlockSpec((None, S, HEAD), lambda b, h: (b, 0, off + h))

    full = pl.BlockSpec((S, HEAD), lambda b, h: (0, 0))
    per_head = lambda r: pl.BlockSpec((None, r, HEAD), lambda b, h: (h, 0, 0))
    return pl.pallas_call(
        _ret_kernel,
        grid=(B, H),
        in_specs=[col(OFF_RQ), col(OFF_RK), col(OFF_RV), col(OFF_RG), full, full,
                  pl.BlockSpec((None, C, C), lambda b, h: (h, 0, 0)),
                  per_head(C), per_head(C), per_head(1), per_head(1)],
        out_specs=pl.BlockSpec((None, S, HEAD), lambda b, h: (b, 0, h)),
        out_shape=jax.ShapeDtypeStruct((B, S, H * HEAD), BF16),
        scratch_shapes=[pltpu.VMEM((HEAD, HEAD), F32)],
        compiler_params=_params(("parallel", "parallel")),
        name="retention",
    )(proj, proj, proj, proj, cosf, sinf, intra, kdec, qdec, cdec, gain)


def _dil_kernel(q_ref, k_ref, v_ref, o_ref, acc, m, s):
    S = q_ref.shape[0]
    Q = DIL_BLOCK
    scale = HEAD ** -0.5
    neg = jnp.finfo(F32).min
    ii = lax.broadcasted_iota(jnp.int32, (Q, 2 * Q), 0)
    jj = lax.broadcasted_iota(jnp.int32, (Q, 2 * Q), 1)
    cur_ok = jnp.logical_and(jj >= Q, jj - Q <= ii)
    prev_ok = jnp.logical_and(jj < Q, jj >= ii)
    order = tuple(reversed(DIL_DILATIONS))

    for pi, d in enumerate(order):
        nb = S // (d * Q)
        shift = nb.bit_length() - 1
        first, last = pi == 0, pi == len(order) - 1

        def load(ref, start, d=d):
            if d == 1:
                return ref[pl.ds(pl.multiple_of(start, Q), Q), :]
            return ref.at[pl.ds(start, (Q - 1) * d + 1), :][pl.ds(0, Q, stride=d), :]

        def store(ref, start, val, d=d):
            if d == 1:
                ref[pl.ds(pl.multiple_of(start, Q), Q), :] = val
            else:
                ref.at[pl.ds(start, (Q - 1) * d + 1), :][pl.ds(0, Q, stride=d), :] = val

        def pair(jt, carry, d=d, nb=nb, shift=shift, first=first, last=last, load=load, store=store):
            it = 2 * jt
            n0 = it & (nb - 1)
            r = it >> shift
            start = r + d * Q * n0
            has_prev = n0 > 0
            starts = (jnp.maximum(start - d * Q, r), start, start + d * Q)
            ks = [load(k_ref, st).astype(BF16) for st in starts]
            vs = [load(v_ref, st).astype(BF16) for st in starts]
            for b in range(2):
                qs = starts[b + 1]
                q = load(q_ref, qs).astype(BF16)
                sc = _dot_nt(q, jnp.concatenate([ks[b], ks[b + 1]], axis=0)) * scale
                ok = jnp.logical_or(cur_ok, jnp.logical_and(prev_ok, has_prev)) if b == 0 else jnp.logical_or(cur_ok, prev_ok)
                sc = jnp.where(ok, sc, neg)
                mx = jnp.max(sc, axis=-1, keepdims=True)
                p = jnp.exp(sc - mx)
                a = _dot(p.astype(BF16), jnp.concatenate([vs[b], vs[b + 1]], axis=0))
                m_new = jnp.broadcast_to(mx, (Q, HEAD))
                s_new = jnp.broadcast_to(jnp.sum(p, axis=-1, keepdims=True), (Q, HEAD))
                if first:
                    store(acc, qs, a)
                    store(m, qs, m_new)
                    store(s, qs, s_new)
                    continue
                m_old = load(m, qs)
                m_all = jnp.maximum(m_old, m_new)
                e_old = jnp.exp(m_old - m_all)
                e_new = jnp.exp(m_new - m_all)
                a = load(acc, qs) * e_old + a * e_new
                s_all = load(s, qs) * e_old + s_new * e_new
                if last:
                    store(o_ref, qs, (a / s_all).astype(o_ref.dtype))
                else:
                    store(acc, qs, a)
                    store(m, qs, m_all)
                    store(s, qs, s_all)
            return carry

        lax.fori_loop(0, S // (2 * Q), pair, 0, unroll=8)


def _dilated(proj):
    B, S, _ = proj.shape
    H = DIL_HEADS

    def col(off):
        return pl.BlockSpec((None, S, HEAD), lambda b, h: (b, 0, off + h))

    return pl.pallas_call(
        _dil_kernel,
        grid=(B, H),
        in_specs=[col(OFF_DQ), col(OFF_DK), col(OFF_DV)],
        out_specs=pl.BlockSpec((None, S, HEAD), lambda b, h: (b, 0, h)),
        out_shape=jax.ShapeDtypeStruct((B, S, H * HEAD), BF16),
        scratch_shapes=[pltpu.VMEM((S, HEAD), F32)] * 3,
        compiler_params=_params(("parallel", "parallel")),
        name="dilated",
    )(proj, proj, proj)


_GLA_GATE_ROWS = 512


def _gla_kernel(q_ref, k_ref, v_ref, r_ref, ga_ref, wg_ref, bg_ref, gain_ref, o_ref, la_ref, st_ref):
    S = q_ref.shape[0]
    C = GLA_CHUNK
    for c in range(S // _GLA_GATE_ROWS):
        rows = pl.ds(c * _GLA_GATE_ROWS, _GLA_GATE_ROWS)
        z = _dot(ga_ref[rows, :].astype(BF16), wg_ref[...]) + bg_ref[...]
        la_ref[rows, :] = (jnp.minimum(z, 0.0) - jnp.log1p(jnp.exp(-jnp.abs(z)))) * (1.0 / GLA_TAU)
    st_ref[...] = jnp.zeros_like(st_ref)
    row_i = lax.broadcasted_iota(jnp.int32, (C, LANE), 0)
    lane = lax.broadcasted_iota(jnp.int32, (C, LANE), 1)
    tril = lax.broadcasted_iota(jnp.int32, (C, C), 1) <= lax.broadcasted_iota(jnp.int32, (C, C), 0)

    def chunk(n, carry):
        rows = pl.ds(pl.multiple_of(n * C, C), C)
        b = la_ref[rows, :]
        sh = 1
        while sh < C:
            b = b + jnp.where(row_i >= sh, pltpu.roll(b, sh, 0), 0.0)
            sh *= 2
        q = q_ref[rows, :] * (GLA_DK ** -0.5)
        k = k_ref[rows, :]
        b_last = b[C - 1:C, :]
        qe = q * jnp.exp(b)
        ke = (k * jnp.exp(-b)).astype(BF16)
        kd = k * jnp.exp(b_last - b)
        decay = jnp.exp(b_last)
        for hh in range(2):
            mine = jnp.logical_and(lane >= GLA_DK * hh, lane < GLA_DK * (hh + 1))
            cols = pl.ds(HEAD * hh, HEAD)
            qh = jnp.where(mine, qe, 0.0).astype(BF16)
            kh = jnp.where(mine, kd, 0.0).astype(BF16)
            vh = v_ref[rows, cols].astype(BF16)
            a = jnp.where(tril, _dot_nt(qh, ke), 0.0)
            st = st_ref[hh]
            o = _dot(a.astype(BF16), vh) + _dot_nt(qh, st.astype(BF16))
            st_ref[hh] = st * decay + _dot_tn(vh, kh)
            g = r_ref[rows, cols]
            o_ref[rows, cols] = (_silu(g) * (_head_norm(o) * gain_ref[:, cols])).astype(o_ref.dtype)
        return carry

    lax.fori_loop(0, S // C, chunk, 0, unroll=8)


def _gla(proj, wg, bg, gain):
    B, S, _ = proj.shape
    P = GLA_HEADS // 2
    one = lambda off: pl.BlockSpec((None, S, LANE), lambda b, p: (b, 0, off + p))
    two = lambda off: pl.BlockSpec((None, S, 2 * LANE), lambda b, p: (b, 0, off + p))
    return pl.pallas_call(
        _gla_kernel,
        grid=(B, P),
        in_specs=[one(OFF_GQ), one(OFF_GK), two(OFF_GV), two(OFF_GR),
                  pl.BlockSpec((None, S, LANE), lambda b, p: (b, 0, OFF_GA)),
                  pl.BlockSpec((None, LANE, LANE), lambda b, p: (p, 0, 0)),
                  pl.BlockSpec((None, 1, LANE), lambda b, p: (p, 0, 0)),
                  pl.BlockSpec((None, 1, 2 * LANE), lambda b, p: (p, 0, 0))],
        out_specs=pl.BlockSpec((None, S, 2 * LANE), lambda b, p: (b, 0, p)),
        out_shape=jax.ShapeDtypeStruct((B, S, GLA_HEADS * HEAD), BF16),
        scratch_shapes=[pltpu.VMEM((S, LANE), F32), pltpu.VMEM((2, HEAD, LANE), F32)],
        compiler_params=_params(("parallel", "parallel")),
        name="gla",
    )(proj, proj, proj, proj, proj, wg, bg, gain)


def _layer_norm(y, g, b):
    mu = jnp.mean(y, axis=-1, keepdims=True)
    d = y - mu
    var = jnp.mean(d * d, axis=-1, keepdims=True)
    return d * lax.rsqrt(var + LN_EPS) * g + b


SLAB = 16


def _to_slab(ref, val):
    n = val.shape[0]
    for j in range(SLAB):
        ref[pl.ds(j, n, stride=SLAB), :] = val[:, j * LANE:(j + 1) * LANE]


def _from_slab(ref, n):
    return jnp.concatenate([ref[pl.ds(j, n, stride=SLAB), :] for j in range(SLAB)], axis=1)


_POST_SUB = 256


def _post_kernel(ret_ref, dil_ref, gla_ref, x_ref, w_ref, g_ref, b_ref, wr_ref, br_ref, x1_ref, rt_ref,
                 y_even, y_odd):
    i = pl.program_id(0)

    @pl.when(i == 0)
    def _():
        y_odd[...] = jnp.zeros_like(y_odd)

    def step(y_out, y_in):
        n_ret = ret_ref.shape[1]
        n_dil = dil_ref.shape[1]
        mixed = (_dot(ret_ref[...], w_ref[0:n_ret, :])
                 + _dot(dil_ref[...], w_ref[n_ret:n_ret + n_dil, :])
                 + _dot(gla_ref[...], w_ref[n_ret + n_dil:, :]))
        y_out[...] = ALPHA * x_ref[...] + mixed
        _post_rows(y_in, g_ref, b_ref, wr_ref, br_ref, x1_ref, rt_ref)

    @pl.when(i % 2 == 0)
    def _():
        step(y_even, y_odd)

    @pl.when(i % 2 == 1)
    def _():
        step(y_odd, y_even)


def _post_rows(y_ref, g_ref, b_ref, wr_ref, br_ref, x1_ref, rt_ref):
    x1 = _layer_norm(y_ref[...], g_ref[...], b_ref[...])
    _to_slab(x1_ref, x1)

    logits = _dot(x1.astype(BF16), wr_ref[...]) + br_ref[...]
    lane_i = lax.broadcasted_iota(jnp.int32, logits.shape, 1)
    lane = lane_i.astype(F32)
    far = float(LANE)
    ninf = -jnp.inf
    gmask = lane_i < N_GROUPS
    gmax = jnp.max(jnp.where(gmask, logits, ninf), axis=-1, keepdims=True)
    gsel = jnp.min(jnp.where(jnp.logical_and(gmask, logits == gmax), lane, far), axis=-1, keepdims=True)
    g_w = 1.0 / jnp.sum(jnp.where(gmask, jnp.exp(logits - gmax), 0.0), axis=-1, keepdims=True)
    lo = N_GROUPS + EXPERTS_PER_GROUP * gsel
    emask = jnp.logical_and(lane >= lo, lane < lo + EXPERTS_PER_GROUP)
    v1 = jnp.max(jnp.where(emask, logits, ninf), axis=-1, keepdims=True)
    i1 = jnp.min(jnp.where(jnp.logical_and(emask, logits == v1), lane, far), axis=-1, keepdims=True)
    emask2 = jnp.logical_and(emask, lane != i1)
    v2 = jnp.max(jnp.where(emask2, logits, ninf), axis=-1, keepdims=True)
    i2 = jnp.min(jnp.where(jnp.logical_and(emask2, logits == v2), lane, far), axis=-1, keepdims=True)
    t = jnp.exp(v2 - v1)
    gate1 = g_w / (1.0 + t)
    gate2 = g_w * t / (1.0 + t)
    rt_ref[...] = jnp.where(lane_i == 0, i1 - N_GROUPS,
                            jnp.where(lane_i == 1, i2 - N_GROUPS,
                                      jnp.where(lane_i == 2, gate1,
                                                jnp.where(lane_i == 3, gate2, 0.0))))


def _post_attn(ret, dil, gla, x, w, g, b, wr, br):
    T, D = x.shape
    tm = _POST_SUB
    nt = T // tm
    row = lambda n: pl.BlockSpec((tm, n), lambda i: (jnp.minimum(i, nt - 1), 0))
    fix = lambda shape: pl.BlockSpec(shape, lambda i: (0, 0))
    lag = lambda i: jnp.maximum(i - 1, 0)
    return pl.pallas_call(
        _post_kernel,
        grid=(nt + 1,),
        in_specs=[row(ret.shape[1]), row(dil.shape[1]), row(gla.shape[1]), row(D),
                  fix(w.shape), fix((1, D)), fix((1, D)), fix(wr.shape), fix((1, LANE))],
        out_specs=[pl.BlockSpec((tm * SLAB, LANE), lambda i: (lag(i), 0)),
                   pl.BlockSpec((tm, LANE), lambda i: (lag(i), 0))],
        out_shape=[jax.ShapeDtypeStruct((T * SLAB, LANE), F32), jax.ShapeDtypeStruct((T, LANE), F32)],
        scratch_shapes=[pltpu.VMEM((tm, D), F32), pltpu.VMEM((tm, D), F32)],
        compiler_params=_params(("arbitrary",)),
        name="post_attn",
    )(ret, dil, gla, x, w, g, b, wr, br)


def _slab_copy(src_hbm, row, dst, r, sem):
    return pltpu.make_async_copy(src_hbm.at[pl.ds(pl.multiple_of(row * SLAB, SLAB), SLAB), :],
                                 dst.at[pl.ds(r * SLAB, SLAB), :], sem)


def _rank_kernel(rt_ref, rank_ref, count_ref, offset_ref):
    i = pl.program_id(0)
    tm = rt_ref.shape[0]

    @pl.when(i == 0)
    def _():
        offset_ref[...] = jnp.zeros_like(offset_ref)

    rt = rt_ref[...]
    lane_i = lax.broadcasted_iota(jnp.int32, (tm, LANE), 1)
    lane = lane_i.astype(F32)
    hot = [(lane == rt[:, k:k + 1]).astype(F32) for k in range(TOP_K)]
    both = hot[0] + hot[1]
    before = (lax.broadcasted_iota(jnp.int32, (tm, tm), 1) < lax.broadcasted_iota(jnp.int32, (tm, tm), 0))
    seen = _dot(before.astype(BF16), both.astype(BF16)) + offset_ref[...]
    ranks = [jnp.sum(hot[k] * seen, axis=-1, keepdims=True) for k in range(TOP_K)]
    rank_ref[...] = jnp.where(lane_i == 0, ranks[0], jnp.where(lane_i == 1, ranks[1], 0.0))
    offset_ref[...] += jnp.sum(both, axis=0, keepdims=True)
    count_ref[...] = offset_ref[...]


def _rank(rt):
    T = rt.shape[0]
    tm = 512
    return pl.pallas_call(
        _rank_kernel,
        grid=(T // tm,),
        in_specs=[pl.BlockSpec((tm, LANE), lambda i: (i, 0))],
        out_specs=[pl.BlockSpec((tm, LANE), lambda i: (i, 0)), pl.BlockSpec((1, LANE), lambda i: (0, 0))],
        out_shape=[jax.ShapeDtypeStruct((T, LANE), F32), jax.ShapeDtypeStruct((1, LANE), F32)],
        scratch_shapes=[pltpu.VMEM((1, LANE), F32)],
        compiler_params=_params(("arbitrary",)),
        name="rank",
    )(rt)


def _scatter_kernel(pos_ref, x_ref, xs_in, xs_hbm, sem):
    del xs_in
    tm = x_ref.shape[0] // SLAB

    def copies():
        for r in range(tm):
            for k in range(TOP_K):
                row = pl.multiple_of(pos_ref[0, TOP_K * r + k] * SLAB, SLAB)
                yield pltpu.make_async_copy(x_ref.at[pl.ds(r * SLAB, SLAB), :],
                                            xs_hbm.at[pl.ds(row, SLAB), :], sem.at[0])

    for c in copies():
        c.start()
    for c in copies():
        c.wait()


def _scatter_rows(x1, pos, xs):
    T = pos.shape[0]
    tm = 512
    nt = T // tm
    return pl.pallas_call(
        _scatter_kernel,
        grid=(nt,),
        in_specs=[pl.BlockSpec((None, 1, TOP_K * tm), lambda i: (i, 0, 0), memory_space=pltpu.SMEM),
                  pl.BlockSpec((tm * SLAB, LANE), lambda i: (i, 0)),
                  pl.BlockSpec(memory_space=pl.ANY)],
        out_specs=pl.BlockSpec(memory_space=pl.ANY),
        out_shape=jax.ShapeDtypeStruct(xs.shape, xs.dtype),
        scratch_shapes=[pltpu.SemaphoreType.DMA((1,))],
        input_output_aliases={2: 0},
        compiler_params=_params(("arbitrary",)),
        name="scatter_rows",
    )(pos.reshape(nt, 1, TOP_K * tm), x1, xs)


def _expert_kernel(be_ref, xs_ref, wg_ref, wu_ref, wd_ref, ys_ref, wgb, wub, wdb, h_even, h_odd):
    i = pl.program_id(0)
    last = pl.num_programs(0) - 2
    bm = xs_ref.shape[0] // SLAB
    up, down = jnp.minimum(i, last), jnp.maximum(i - 1, 0)

    @pl.when(jnp.logical_or(i == 0, be_ref[up] != be_ref[jnp.maximum(up - 1, 0)]))
    def _():
        wgb[...] = wg_ref[...].astype(BF16)
        wub[...] = wu_ref[...].astype(BF16)

    @pl.when(jnp.logical_or(i <= 1, be_ref[down] != be_ref[jnp.maximum(down - 1, 0)]))
    def _():
        wdb[...] = wd_ref[...].astype(BF16)

    @pl.when(i == 0)
    def _():
        h_odd[...] = jnp.zeros_like(h_odd)

    def step(h_out, h_in):
        xb = _from_slab(xs_ref, bm).astype(BF16)
        h_out[...] = (_silu(_dot(xb, wgb[...])) * _dot(xb, wub[...])).astype(BF16)
        _to_slab(ys_ref, _dot(h_in[...], wdb[...]))

    @pl.when(i % 2 == 0)
    def _():
        step(h_even, h_odd)

    @pl.when(i % 2 == 1)
    def _():
        step(h_odd, h_even)


def _experts(xs, blk_expert, wg, wu, wd, layer):
    D = SLAB * LANE
    F = wg.shape[3]
    nblk = blk_expert.shape[0]
    bm = MOE_BM
    up = lambda i: jnp.minimum(i, nblk - 1)
    down = lambda i: jnp.maximum(i - 1, 0)
    grid_spec = pltpu.PrefetchScalarGridSpec(
        num_scalar_prefetch=1,
        grid=(nblk + 1,),
        in_specs=[
            pl.BlockSpec((bm * SLAB, LANE), lambda i, be: (up(i), 0)),
            pl.BlockSpec((None, None, D, F), lambda i, be: (layer, be[up(i)], 0, 0)),
            pl.BlockSpec((None, None, D, F), lambda i, be: (layer, be[up(i)], 0, 0)),
            pl.BlockSpec((None, None, F, D), lambda i, be: (layer, be[down(i)], 0, 0)),
        ],
        out_specs=pl.BlockSpec((bm * SLAB, LANE), lambda i, be: (down(i), 0)),
        scratch_shapes=[pltpu.VMEM((D, F), BF16), pltpu.VMEM((D, F), BF16), pltpu.VMEM((F, D), BF16),
                        pltpu.VMEM((bm, F), BF16), pltpu.VMEM((bm, F), BF16)],
    )
    return pl.pallas_call(
        _expert_kernel,
        grid_spec=grid_spec,
        out_shape=jax.ShapeDtypeStruct((nblk * bm * SLAB, LANE), F32),
        compiler_params=_params(("arbitrary",)),
        name="experts",
    )(blk_expert, xs, wg, wu, wd)


_COMBINE_AHEAD = 3


def _combine_kernel(*refs):
    pos_refs = refs[:_COMBINE_AHEAD + 1]
    x1_ref, rt_ref, ys_hbm, g_ref, b_ref, x2_ref, x2b_ref, buf, sem = refs[_COMBINE_AHEAD + 1:]
    i = pl.program_id(0)
    n = pl.num_programs(0)
    nslot = _COMBINE_AHEAD + 1
    slot = i % nslot
    tm = rt_ref.shape[0]

    def gather(pos, sl):
        for r in range(tm):
            for k in range(TOP_K):
                _slab_copy(ys_hbm, pos[0, TOP_K * r + k], buf.at[sl, k], r, sem.at[sl]).start()

    def wait(sl):
        for r in range(tm):
            for k in range(TOP_K):
                _slab_copy(ys_hbm, 0, buf.at[sl, k], r, sem.at[sl]).wait()

    @pl.when(i == 0)
    def _():
        for d in range(_COMBINE_AHEAD):
            gather(pos_refs[d], d)

    wait(slot)
    rt = rt_ref[...]
    moe = rt[:, 2:3] * _from_slab(buf.at[slot, 0], tm) + rt[:, 3:4] * _from_slab(buf.at[slot, 1], tm)
    gather(pos_refs[_COMBINE_AHEAD], (i + _COMBINE_AHEAD) % nslot)

    x2 = _layer_norm(ALPHA * _from_slab(x1_ref, tm) + moe, g_ref[...], b_ref[...])
    x2_ref[...] = x2
    x2b_ref[...] = x2.astype(BF16)

    @pl.when(i == n - 1)
    def _():
        for d in range(1, nslot):
            wait((i + d) % nslot)


def _combine(x1, rt, pos, ys, g, b):
    T = rt.shape[0]
    D = SLAB * LANE
    tm = 128
    nt = T // tm
    posr = pos.reshape(nt, 1, TOP_K * tm)
    row = lambda n: pl.BlockSpec((tm, n), lambda i: (i, 0))
    fix = lambda shape: pl.BlockSpec(shape, lambda i: (0, 0))
    ahead = lambda d: pl.BlockSpec((None, 1, TOP_K * tm), lambda i: (jnp.minimum(i + d, nt - 1), 0, 0),
                                   memory_space=pltpu.SMEM)
    return pl.pallas_call(
        _combine_kernel,
        grid=(nt,),
        in_specs=[ahead(d) for d in range(_COMBINE_AHEAD + 1)] + [
            pl.BlockSpec((tm * SLAB, LANE), lambda i: (i, 0)), row(LANE),
            pl.BlockSpec(memory_space=pl.ANY), fix((1, D)), fix((1, D)),
        ],
        out_specs=[row(D), row(D)],
        out_shape=[jax.ShapeDtypeStruct((T, D), F32), jax.ShapeDtypeStruct((T, D), BF16)],
        scratch_shapes=[pltpu.VMEM((_COMBINE_AHEAD + 1, TOP_K, tm * SLAB, LANE), F32),
                        pltpu.SemaphoreType.DMA((_COMBINE_AHEAD + 1,))],
        compiler_params=_params(("arbitrary",)),
        name="combine",
    )(*([posr] * (_COMBINE_AHEAD + 1)), x1, rt, ys, g, b)


def _rope_tables(S):
    half = HEAD // 2
    inv = ROPE_BASE ** (-jnp.arange(half, dtype=F32) / half)
    ang = jnp.arange(S).astype(F32)[:, None] * inv[None, :]
    cos, sin = jnp.cos(ang), jnp.sin(ang)
    return jnp.concatenate([cos, cos], axis=-1), jnp.concatenate([-sin, sin], axis=-1)


def _ret_tables():
    H, C = RET_HEADS, RET_CHUNK
    log_g = jnp.log1p(-jnp.exp2(-5.0 - jnp.arange(H, dtype=F32)))
    idx = jnp.arange(C, dtype=F32)
    diff = idx[:, None] - idx[None, :]
    intra = jnp.where(diff >= 0, jnp.exp(jnp.maximum(diff, 0.0)[None] * log_g[:, None, None]), 0.0)
    kdec = jnp.exp((C - 1 - idx)[None] * log_g[:, None])
    qdec = jnp.exp((idx + 1)[None] * log_g[:, None])
    cdec = jnp.exp(C * log_g)
    wide = lambda t: jnp.broadcast_to(t[:, :, None], t.shape + (HEAD,))
    return intra, wide(kdec), wide(qdec), wide(cdec[:, None])


def _moe_blocks(n_tok):
    return n_tok * TOP_K // MOE_BM + N_EXPERTS


def _dispatch(rt, rank, counts):
    bm = MOE_BM
    nblk = _moe_blocks(rt.shape[0])
    counts = counts[0, :N_EXPERTS].astype(jnp.int32)
    pcounts = (counts + bm - 1) // bm * bm
    pends = jnp.cumsum(pcounts)
    pstarts = pends - pcounts
    ids = rt[:, :TOP_K].astype(jnp.int32)
    hot = ids[:, :, None] == jnp.arange(N_EXPERTS, dtype=jnp.int32)[None, None, :]
    pos = jnp.sum(jnp.where(hot, pstarts[None, None, :], 0), axis=-1) + rank[:, :TOP_K].astype(jnp.int32)
    blk_start = jnp.arange(nblk, dtype=jnp.int32) * bm
    blk_expert = jnp.sum((pends[None, :] <= blk_start[:, None]).astype(jnp.int32), axis=1)
    return pos, jnp.minimum(blk_expert, N_EXPERTS - 1)


def kernel(x, w_in, w_gla_gate, b_gla_gate, ret_norm_g, gla_norm_g, w_out, ln1_g, ln1_b,
           w_router_group, b_router_group, w_router_expert, b_router_expert,
           w_expert_gate, w_expert_up, w_expert_down, ln2_g, ln2_b):
    B, S, D = x.shape
    T = B * S
    assert w_in.shape[2] == PROJ_REAL and S % (DIL_BLOCK * DIL_DILATIONS[-1]) == 0
    cosf, sinf = _rope_tables(S)
    tables = _ret_tables()
    gate_rank = w_gla_gate.shape[1]
    pairs = GLA_HEADS // 2
    xs = jnp.zeros((_moe_blocks(T) * MOE_BM * SLAB, LANE), F32)

    xf = x.reshape(T, D)
    xb = xf.astype(BF16)
    for l in range(DEPTH):
        w_in_p = jnp.pad(w_in[l].astype(BF16), ((0, 0), (0, PROJ_PAD - PROJ_REAL)))
        proj = _in_proj(xb, w_in_p).reshape(B, S, PROJ_PAD)

        ret = _retention(proj, cosf, sinf, tables, ret_norm_g[l].reshape(RET_HEADS, 1, HEAD))
        dil = _dilated(proj)
        wg = w_gla_gate[l].reshape(gate_rank, pairs, LANE).transpose(1, 0, 2)
        wg = jnp.pad(wg.astype(BF16), ((0, 0), (0, LANE - gate_rank), (0, 0)))
        gla = _gla(proj, wg, b_gla_gate[l].reshape(pairs, 1, LANE),
                   gla_norm_g[l].reshape(pairs, 1, 2 * HEAD))

        wr = jnp.concatenate([w_router_group[l], w_router_expert[l]], axis=1)
        wr = jnp.pad(wr.astype(BF16), ((0, 0), (0, LANE - wr.shape[1])))
        br = jnp.concatenate([b_router_group[l], b_router_expert[l]])
        br = jnp.pad(br, (0, LANE - br.shape[0])).reshape(1, LANE)
        x1, rt = _post_attn(ret.reshape(T, -1), dil.reshape(T, -1), gla.reshape(T, -1), xf,
                            w_out[l].astype(BF16), ln1_g[l].reshape(1, D), ln1_b[l].reshape(1, D), wr, br)

        rank, counts = _rank(rt)
        pos, blk_expert = _dispatch(rt, rank, counts)
        xs = _scatter_rows(x1, pos, xs)
        ys = _experts(xs, blk_expert, w_expert_gate, w_expert_up, w_expert_down, l)
        xf, xb = _combine(x1, rt, pos, ys, ln2_g[l].reshape(1, D), ln2_b[l].reshape(1, D))
    return xf.reshape(B, S, D)
```
